```python
import math
import jax, jax.numpy as jnp
from jax import lax
import numpy as np

D_MODEL = 4096
BATCH = 4
SEQ = 2048
DEPTH = 4
DEC_BATCH = 8
DEC_SEQ = 4
PAST_LEN = 8192
PAGE_SIZE = 128

N_META = 16
HEAD_DIM = 128
HG_WIDTH = D_MODEL // 4
DIFF_WIDTH = D_MODEL // 4
SB_WIDTH = D_MODEL - HG_WIDTH - DIFF_WIDTH
HG_HEADS = HG_WIDTH // HEAD_DIM
HG_KDIM = HEAD_DIM
HG_VDIM = HEAD_DIM
HG_CHUNK = 64
DIFF_HEADS = DIFF_WIDTH // HEAD_DIM
DIFF_QK_DIM = HEAD_DIM // 2
DIFF_V_DIM = HEAD_DIM
SB_HEADS = SB_WIDTH // HEAD_DIM
SB_DIM = HEAD_DIM
ROPE_DIM = DIFF_QK_DIM // 4
ROPE_THETA = 500000.0
D_FF = -(-(8 * D_MODEL) // (3 * 256)) * 256
Q_BLOCK = 128
EPS = 1e-6
NEG_BIG = -1e30
COL_WIDTHS = (HG_HEADS * HG_KDIM, HG_HEADS * HG_KDIM, HG_HEADS * HG_VDIM, HG_HEADS * HG_VDIM,
              2 * DIFF_HEADS * DIFF_QK_DIM, 2 * DIFF_HEADS * DIFF_QK_DIM, DIFF_HEADS * DIFF_V_DIM,
              SB_HEADS * SB_DIM, SB_HEADS * SB_DIM, SB_HEADS * SB_DIM)
IN_COLS = sum(COL_WIDTHS)
SPLIT_POINTS = tuple(sum(COL_WIDTHS[:i + 1]) for i in range(len(COL_WIDTHS) - 1))
MIX_WIDTH = HG_HEADS * HG_VDIM + DIFF_HEADS * DIFF_V_DIM + SB_HEADS * SB_DIM
F32 = jnp.float32

kernel_name = 'hymba_hgrn2_diffattn_stickbreak_step'


def _rms_norm(x, g):
    xf = x.astype(F32)
    y = xf * lax.rsqrt(jnp.mean(xf * xf, axis=-1, keepdims=True) + EPS)
    return (y * g.astype(F32)).astype(x.dtype)


def _head_rms_norm(x, g):
    return x * lax.rsqrt(jnp.mean(x * x, axis=-1, keepdims=True) + EPS) * g.astype(F32)


def _project(hn, w_in_l):
    B, T, _ = hn.shape
    z = jnp.einsum('btd,dc->btc', hn, w_in_l)
    hq, hf, hi, hg, dq, dk, dv, sq, sk, sv = jnp.split(z, SPLIT_POINTS, axis=-1)
    r = lambda a, h: a.reshape(B, T, h, -1)
    return (r(hq, HG_HEADS), r(hf, HG_HEADS), r(hi, HG_HEADS), r(hg, HG_HEADS),
            r(dq, 2 * DIFF_HEADS), r(dk, 2 * DIFF_HEADS), r(dv, DIFF_HEADS),
            r(sq, SB_HEADS), r(sk, SB_HEADS), r(sv, SB_HEADS))


def _hgrn_gates(hq, hf, hi, lb):
    lbf = lb.reshape(HG_HEADS, HG_KDIM)
    zf = hf.astype(F32)
    q = jax.nn.silu(hq.astype(F32)) * (HG_KDIM ** -0.5)
    gl = jnp.log(lbf + (1.0 - lbf) * jax.nn.sigmoid(zf))
    k = (1.0 - lbf) * jax.nn.sigmoid(-zf)
    return q, k, hi.astype(F32), gl


def _gla_chunk(S, q, k, v, gl):
    C = q.shape[1]
    G = jnp.cumsum(gl, axis=1)
    causal = jnp.tril(jnp.ones((C, C), dtype=bool))[None, :, :, None, None]
    decay = jnp.exp(jnp.where(causal, G[:, :, None] - G[:, None, :], NEG_BIG))
    attn = jnp.einsum('bthk,bshk,btshk->bhts', q, k, decay)
    o = (jnp.einsum('bthk,bhkv->bthv', q * jnp.exp(G), S)
         + jnp.einsum('bhts,bshv->bthv', attn, v))
    G_last = G[:, -1]
    k_dec = k * jnp.exp(G_last[:, None] - G)
    S_new = jnp.exp(G_last)[..., None] * S + jnp.einsum('bshk,bshv->bhkv', k_dec, v)
    return S_new, o


def _rope(x, pos):
    half = ROPE_DIM // 2
    inv = ROPE_THETA ** (-jnp.arange(half, dtype=F32) / half)
    ang = pos.astype(F32)[:, None] * inv[None, :]
    cos = jnp.cos(ang)[None, :, None, :]
    sin = jnp.sin(ang)[None, :, None, :]
    xr = x[..., :ROPE_DIM].astype(F32)
    x1, x2 = xr[..., :half], xr[..., half:]
    rot = jnp.concatenate([x1 * cos - x2 * sin, x2 * cos + x1 * sin], axis=-1)
    return jnp.concatenate([rot.astype(x.dtype), x[..., ROPE_DIM:]], axis=-1)


def _diff_attend(q, k, v, q_pos, k_pos, lam):
    B, T = q.shape[:2]
    s = jnp.einsum('bthd,bshd->bhts', q.astype(F32), k.astype(F32)) * (DIFF_QK_DIM ** -0.5)
    s = jnp.where((k_pos[None, :] <= q_pos[:, None])[None, None], s, NEG_BIG)
    p = jax.nn.softmax(s, axis=-1).reshape(B, DIFF_HEADS, 2, T, -1)
    w = p[:, :, 0] - lam * p[:, :, 1]
    return jnp.einsum('bhts,bshv->bthv', w, v.astype(F32))


def _sb_attend(q, k, v, q_pos, k_pos):
    z = jnp.einsum('bthd,bshd->bhts', q.astype(F32), k.astype(F32)) * (SB_DIM ** -0.5)
    mask = (k_pos[None, :] < q_pos[:, None])[None, None]
    log_beta = jax.nn.log_sigmoid(z)
    log_1mb = jnp.where(mask, jax.nn.log_sigmoid(-z), 0.0)
    tail = lax.cumsum(log_1mb, axis=3, reverse=True) - log_1mb
    a = jnp.exp(jnp.where(mask, log_beta + tail, NEG_BIG))
    return jnp.einsum('bhts,bshd->bthd', a, v.astype(F32))


def _query_blocks(fn, q, q_pos):
    B, T = q.shape[:2]
    nb = -(-T // Q_BLOCK)
    pad = nb * Q_BLOCK - T
    qp = jnp.pad(q, ((0, 0), (0, pad), (0, 0), (0, 0)))
    pp = jnp.concatenate([q_pos, q_pos[-1] + 1 + jnp.arange(pad, dtype=q_pos.dtype)])
    qb = qp.reshape(B, nb, Q_BLOCK, *q.shape[2:]).swapaxes(0, 1)
    out = lax.map(lambda a: fn(a[0], a[1]), (qb, pp.reshape(nb, Q_BLOCK)))
    return out.swapaxes(0, 1).reshape(B, nb * Q_BLOCK, *out.shape[3:])[:, :T]


def _diff_lambda(lam_l, layer):
    lam_init = 0.8 - 0.6 * math.exp(-0.3 * layer)
    lp = lam_l.astype(F32)
    lam = jnp.exp(jnp.sum(lp[0] * lp[1])) - jnp.exp(jnp.sum(lp[2] * lp[3])) + lam_init
    return lam, lam_init


def _merge(hg_o, diff_o, sb_o, w_out_l, dtype):
    B, T = hg_o.shape[:2]
    m = jnp.concatenate([hg_o.reshape(B, T, -1), diff_o.reshape(B, T, -1),
                         sb_o.reshape(B, T, -1)], axis=-1).astype(dtype)
    return jnp.einsum('btm,md->btd', m, w_out_l)


def _swiglu(h, w_gu, w_dn):
    g, u = jnp.split(jnp.einsum('btd,df->btf', h, w_gu), 2, axis=-1)
    return jnp.einsum('btf,fd->btd', jax.nn.silu(g) * u, w_dn)


def setup_inputs(seed: int = 0) -> dict:
    key = jax.random.key(seed)
    ks = jax.random.split(key, 24)
    n_pages = PAST_LEN // PAGE_SIZE
    n_used = DEC_BATCH * n_pages
    n_phys = n_used + max(1, n_used // 4)
    nrm = lambda k, shape, scale=1.0: scale * jax.random.normal(k, shape, F32)
    page_table = jax.random.permutation(ks[7], n_phys)[:n_used].reshape(DEC_BATCH, n_pages).astype(jnp.int32)
    return {
        'x_prompt': nrm(ks[0], (BATCH, SEQ, D_MODEL)),
        'x_sample': nrm(ks[1], (DEC_BATCH, DEC_SEQ, D_MODEL)),
        'cache_diff_k': nrm(ks[2], (DEPTH, n_phys, PAGE_SIZE, 2 * DIFF_HEADS, DIFF_QK_DIM)),
        'cache_diff_v': nrm(ks[3], (DEPTH, n_phys, PAGE_SIZE, DIFF_HEADS, DIFF_V_DIM)),
        'cache_sb_k': nrm(ks[4], (DEPTH, n_phys, PAGE_SIZE, SB_HEADS, SB_DIM)),
        'cache_sb_v': nrm(ks[5], (DEPTH, n_phys, PAGE_SIZE, SB_HEADS, SB_DIM)),
        'state_hgrn': nrm(ks[6], (DEPTH, DEC_BATCH, HG_HEADS, HG_KDIM, HG_VDIM), 0.5),
        'page_table': page_table,
        'meta_tokens': nrm(ks[8], (N_META, D_MODEL)),
        'norm_mix': 1.0 + nrm(ks[9], (DEPTH, D_MODEL), 0.02),
        'norm_ffn': 1.0 + nrm(ks[10], (DEPTH, D_MODEL), 0.02),
        'norm_final': 1.0 + nrm(ks[11], (D_MODEL,), 0.02),
        'w_in': nrm(ks[12], (DEPTH, D_MODEL, IN_COLS), D_MODEL ** -0.5),
        'hg_lower_bounds': nrm(ks[13], (DEPTH, HG_HEADS * HG_KDIM), 0.1),
        'hg_norm': 1.0 + nrm(ks[14], (DEPTH, HG_VDIM), 0.02),
        'diff_lambda': nrm(ks[15], (DEPTH, 4, DIFF_QK_DIM), 0.1),
        'diff_subln': 1.0 + nrm(ks[16], (DEPTH, DIFF_V_DIM), 0.02),
        'sb_norm': 1.0 + nrm(ks[17], (DEPTH, SB_DIM), 0.02),
        'w_out': nrm(ks[18], (DEPTH, MIX_WIDTH, D_MODEL), MIX_WIDTH ** -0.5),
        'w_gate_up': nrm(ks[19], (DEPTH, D_MODEL, 2 * D_FF), D_MODEL ** -0.5),
        'w_down': nrm(ks[20], (DEPTH, D_FF, D_MODEL), D_FF ** -0.5),
    }


def reference(x_prompt, x_sample, cache_diff_k, cache_diff_v, cache_sb_k, cache_sb_v, state_hgrn,
              page_table, meta_tokens, norm_mix, norm_ffn, norm_final, w_in, hg_lower_bounds, hg_norm,
              diff_lambda, diff_subln, sb_norm, w_out, w_gate_up, w_down):
    dt = x_prompt.dtype
    B = x_prompt.shape[0]
    DB = x_sample.shape[0]
    L = SEQ + N_META
    xp = jnp.concatenate([jnp.broadcast_to(meta_tokens.astype(dt)[None], (B, N_META, D_MODEL)), x_prompt], axis=1)
    xs = x_sample
    pos_p = jnp.arange(L, dtype=jnp.int32)
    pos_s = PAST_LEN + jnp.arange(DEC_SEQ, dtype=jnp.int32)
    kpos_s = jnp.arange(PAST_LEN + DEC_SEQ, dtype=jnp.int32)
    lb_sm = jax.nn.softmax(hg_lower_bounds.astype(F32), axis=0)
    lower_bounds = jnp.cumsum(lb_sm, axis=0) - lb_sm[0]
    pk_d, pv_d, pk_s, pv_s, ps_h = [], [], [], [], []
    sk_d, sv_d, sk_s, sv_s, ss_h = [], [], [], [], []
    for l in range(DEPTH):
        lam, lam_init = _diff_lambda(diff_lambda[l], l)
        gather = lambda c: c[l, page_table].reshape(DB, PAST_LEN, *c.shape[3:])

        hn = _rms_norm(xp, norm_mix[l])
        hq, hf, hi, hg, dq, dk, dv, sq, sk, sv = _project(hn, w_in[l])
        q, k, v, gl = _hgrn_gates(hq, hf, hi, lower_bounds[l])
        S0 = jnp.zeros((B, HG_HEADS, HG_KDIM, HG_VDIM), F32)
        S_meta, o_meta = _gla_chunk(S0, q[:, :N_META], k[:, :N_META], v[:, :N_META], gl[:, :N_META])
        chunks = lambda a: a[:, N_META:].reshape(B, SEQ // HG_CHUNK, HG_CHUNK, *a.shape[2:]).swapaxes(0, 1)
        S_fin, o_rest = lax.scan(lambda S, c: _gla_chunk(S, *c), S_meta,
                                 (chunks(q), chunks(k), chunks(v), chunks(gl)))
        o_hg = jnp.concatenate([o_meta, o_rest.swapaxes(0, 1).reshape(B, SEQ, HG_HEADS, HG_VDIM)], axis=1)
        hg_o = _head_rms_norm(o_hg, hg_norm[l]) * jax.nn.silu(hg.astype(F32))
        dq = _rope(dq, pos_p)
        dk = _rope(dk, pos_p)
        diff_o = _query_blocks(lambda qb, pb: _diff_attend(qb, dk, dv, pb, pos_p, lam), dq, pos_p)
        diff_o = _head_rms_norm(diff_o, diff_subln[l]) * (1.0 - lam_init)
        sb_o = _query_blocks(lambda qb, pb: _sb_attend(qb, sk, sv, pb, pos_p), sq, pos_p)
        sb_o = _head_rms_norm(sb_o, sb_norm[l])
        xp = xp + _merge(hg_o, diff_o, sb_o, w_out[l], dt)
        xp = xp + _swiglu(_rms_norm(xp, norm_ffn[l]), w_gate_up[l], w_down[l])
        pk_d.append(dk)
        pv_d.append(dv)
        pk_s.append(sk)
        pv_s.append(sv)
        ps_h.append(S_fin.astype(state_hgrn.dtype))

        hn = _rms_norm(xs, norm_mix[l])
        hq, hf, hi, hg, dq, dk, dv, sq, sk, sv = _project(hn, w_in[l])
        q, k, v, gl = _hgrn_gates(hq, hf, hi, lower_bounds[l])
        S_s, o_hg = _gla_chunk(state_hgrn[l].astype(F32), q, k, v, gl)
        hg_o = _head_rms_norm(o_hg, hg_norm[l]) * jax.nn.silu(hg.astype(F32))
        dq = _rope(dq, pos_s)
        dk = _rope(dk, pos_s)
        dk_all = jnp.concatenate([gather(cache_diff_k).astype(dk.dtype), dk], axis=1)
        dv_all = jnp.concatenate([gather(cache_diff_v).astype(dv.dtype), dv], axis=1)
        diff_o = _diff_attend(dq, dk_all, dv_all, pos_s, kpos_s, lam)
        diff_o = _head_rms_norm(diff_o, diff_subln[l]) * (1.0 - lam_init)
        sk_all = jnp.concatenate([gather(cache_sb_k).astype(sk.dtype), sk], axis=1)
        sv_all = jnp.concatenate([gather(cache_sb_v).astype(sv.dtype), sv], axis=1)
        sb_o = _head_rms_norm(_sb_attend(sq, sk_all, sv_all, pos_s, kpos_s), sb_norm[l])
        xs = xs + _merge(hg_o, diff_o, sb_o, w_out[l], xs.dtype)
        xs = xs + _swiglu(_rms_norm(xs, norm_ffn[l]), w_gate_up[l], w_down[l])
        sk_d.append(dk)
        sv_d.append(dv)
        sk_s.append(sk)
        sv_s.append(sv)
        ss_h.append(S_s.astype(state_hgrn.dtype))

    y_prompt = _rms_norm(xp, norm_final)[:, N_META:]
    y_sample = _rms_norm(xs, norm_final)
    return (y_prompt, y_sample,
            jnp.stack(pk_d, 0), jnp.stack(pv_d, 0), jnp.stack(pk_s, 0), jnp.stack(pv_s, 0), jnp.stack(ps_h, 0),
            jnp.stack(sk_d, 0), jnp.stack(sv_d, 0), jnp.stack(sk_s, 0), jnp.stack(sv_s, 0), jnp.stack(ss_h, 0))
```

```python
import functools
import math

import jax
import jax.numpy as jnp
from jax import lax
from jax.experimental import pallas as pl
from jax.experimental.pallas import tpu as pltpu

F32 = jnp.float32
BF16 = jnp.bfloat16
EPS = 1e-6
NEG_BIG = -1e30
ROPE_THETA = 500000.0
LANES = 128
SAMPLE_ROWS = 8
V7X_VMEM_BYTES = 64 * 1024 * 1024
VMEM_CAP = V7X_VMEM_BYTES - 8 * 1024 * 1024


def _params(semantics, vmem_bytes):
    limit = int(min(max(vmem_bytes, 16 * 1024 * 1024), VMEM_CAP))
    return pltpu.CompilerParams(dimension_semantics=semantics, vmem_limit_bytes=limit)


def _divisor_tile(n, target, multiple):
    best = None
    for t in range(multiple, min(n, target) + 1, multiple):
        if n % t == 0:
            best = t
    return best if best is not None else n


def _nt_dot(a, b):
    return lax.dot_general(a, b, (((1,), (1,)), ((), ())), preferred_element_type=F32)


def _dot(a, b):
    return jnp.dot(a, b, preferred_element_type=F32)


def _iota(shape, dim):
    return lax.broadcasted_iota(jnp.int32, shape, dim)


def _split_bf16(x, parts):
    out = []
    r = x
    for i in range(parts):
        p = r.astype(BF16)
        out.append(p)
        if i + 1 < parts:
            r = r - p.astype(F32)
    return out


def _sigmoid_pair(z):
    e = jnp.exp(-jnp.abs(z))
    r = 1.0 / (1.0 + e)
    er = e * r
    pos = z >= 0
    return jnp.where(pos, r, er), jnp.where(pos, er, r)


def _silu(z):
    return z * _sigmoid_pair(z)[0]


def _head_norm(o, g):
    return o * lax.rsqrt(jnp.mean(o * o, axis=-1, keepdims=True) + EPS) * g


def _rmsnorm_kernel(x_ref, g_ref, o_ref):
    x = x_ref[...]
    y = x * lax.rsqrt(jnp.mean(x * x, axis=-1, keepdims=True) + EPS)
    o_ref[...] = (y * g_ref[...]).astype(o_ref.dtype)


def _rmsnorm(x, gains, layer, out_dtype):
    M, D = x.shape
    tr = _divisor_tile(M, 256, 16)
    vmem = 2 * tr * D * (4 + jnp.dtype(out_dtype).itemsize) + 3 * tr * D * 4
    return pl.pallas_call(
        _rmsnorm_kernel,
        grid=(M // tr,),
        in_specs=[pl.BlockSpec((tr, D), lambda i: (i, 0)),
                  pl.BlockSpec((None, 1, D), lambda i: (layer, 0, 0))],
        out_specs=pl.BlockSpec((tr, D), lambda i: (i, 0)),
        out_shape=jax.ShapeDtypeStruct((M, D), out_dtype),
        compiler_params=_params(("parallel",), vmem),
        name="rmsnorm",
    )(x, gains)


def _rope_kernel(x_ref, c_ref, s1_ref, s2_ref, o_ref):
    c = c_ref[...]
    s1 = s1_ref[...]
    s2 = s2_ref[...]
    for j in range(x_ref.shape[1] // LANES):
        x = x_ref[:, j * LANES:(j + 1) * LANES]
        up = pltpu.roll(x, LANES - 8, 1)
        dn = pltpu.roll(x, 8, 1)
        o_ref[:, j * LANES:(j + 1) * LANES] = x * c + up * s1 + dn * s2


def _rope(z, col_block, width, tables):
    M = z.shape[0]
    tr = _divisor_tile(M, 512, 8)
    tab = pl.BlockSpec((tr, LANES), lambda i: (i, 0))
    vmem = 4 * tr * width * 4 + 6 * tr * LANES * 4 + 4 * tr * LANES * 4
    return pl.pallas_call(
        _rope_kernel,
        grid=(M // tr,),
        in_specs=[pl.BlockSpec((tr, width), lambda i: (i, col_block)), tab, tab, tab],
        out_specs=pl.BlockSpec((tr, width), lambda i: (i, 0)),
        out_shape=jax.ShapeDtypeStruct((M, width), F32),
        compiler_params=_params(("parallel",), vmem),
        name="rope",
    )(z, *tables)


def _matmul_kernel(a_ref, w_ref, *rest, nk, has_res):
    o_ref = rest[-1]
    p = _dot(a_ref[...], w_ref[...])
    if has_res:
        first = p + rest[0][...]
    else:
        first = p
    if nk == 1:
        o_ref[...] = first
    else:
        k = pl.program_id(2)

        @pl.when(k == 0)
        def _():
            o_ref[...] = first

        @pl.when(k > 0)
        def _():
            o_ref[...] += p


def _matmul_tiles(M, K, N):
    bk = K if K <= 4096 else _divisor_tile(K, 5632, LANES)
    bm = _divisor_tile(M, 1664 if bk == K else 832, 16)
    bn = _divisor_tile(N, 512, LANES)
    return bm, bk, bn


def _matmul(a, w, layer, res=None):
    M, K = a.shape
    N = w.shape[2]
    bm, bk, bn = _matmul_tiles(M, K, N)
    nk = K // bk
    a_bufs = 1 if nk == 1 else 2
    a_kw = dict(pipeline_mode=pl.Buffered(1)) if nk == 1 else {}
    in_specs = [pl.BlockSpec((bm, bk), lambda i, j, k: (i, k), **a_kw),
                pl.BlockSpec((None, bk, bn), lambda i, j, k: (layer, k, j))]
    args = [a, w]
    if res is not None:
        in_specs.append(pl.BlockSpec((bm, bn), lambda i, j, k: (i, j)))
        args.append(res)
    vmem = (a_bufs * bm * bk * 2 + 2 * bk * bn * 2 + (4 if res is not None else 2) * bm * bn * 4
            + 2 * bm * bn * 4)
    return pl.pallas_call(
        functools.partial(_matmul_kernel, nk=nk, has_res=res is not None),
        grid=(M // bm, N // bn, nk),
        in_specs=in_specs,
        out_specs=pl.BlockSpec((bm, bn), lambda i, j, k: (i, j)),
        out_shape=jax.ShapeDtypeStruct((M, N), F32),
        compiler_params=_params(("parallel", "parallel", "arbitrary"), vmem),
        name="matmul",
    )(*args)


def _ffn_up_kernel(a_ref, wg_ref, wu_ref, o_ref):
    a = a_ref[...]
    g = _dot(a, wg_ref[...])
    u = _dot(a, wu_ref[...])
    o_ref[...] = (_silu(g) * u).astype(o_ref.dtype)


def _ffn_up(a, w_gu, layer):
    M, K = a.shape
    F = w_gu.shape[2] // 2
    bm = _divisor_tile(M, 1664, 16)
    bn = _divisor_tile(F, 256, LANES)
    nf = F // bn
    vmem = bm * K * 2 + 4 * K * bn * 2 + 2 * bm * bn * 2 + 4 * bm * bn * 4
    return pl.pallas_call(
        _ffn_up_kernel,
        grid=(M // bm, nf),
        in_specs=[pl.BlockSpec((bm, K), lambda i, j: (i, 0), pipeline_mode=pl.Buffered(1)),
                  pl.BlockSpec((None, K, bn), lambda i, j: (layer, 0, j)),
                  pl.BlockSpec((None, K, bn), lambda i, j: (layer, 0, nf + j))],
        out_specs=pl.BlockSpec((bm, bn), lambda i, j: (i, j)),
        out_shape=jax.ShapeDtypeStruct((M, F), BF16),
        compiler_params=_params(("parallel", "parallel"), vmem),
        name="ffn_up",
    )(a, w_gu, w_gu)


def _lower_bound(lbraw_ref, layer):
    raw = lbraw_ref[...]
    e = jnp.exp(raw - jnp.max(raw, axis=0, keepdims=True))
    total = jnp.sum(e, axis=0, keepdims=True)
    if layer == 0:
        return jnp.zeros_like(total)
    return jnp.sum(e[1:layer + 1], axis=0, keepdims=True) / total


def _gla_chunk(st, hq, hf, hi, lb, *, valid_rows, sub):
    C, K = hq.shape
    sig, sig_neg = _sigmoid_pair(hf)
    q = _silu(hq) * (K ** -0.5)
    gl = jnp.log(lb + (1.0 - lb) * sig)
    k = (1.0 - lb) * sig_neg
    v = hi
    if valid_rows < C:
        ok = _iota((C, 1), 0) < valid_rows
        gl = jnp.where(ok, gl, 0.0)
        k = jnp.where(ok, k, 0.0)

    row = _iota((C, C), 0)
    col = _iota((C, C), 1)
    tril = (col <= row).astype(BF16)
    G = None
    for part in _split_bf16(gl, 3):
        t = _dot(tril, part)
        G = t if G is None else G + t

    o = _nt_dot((q * jnp.exp(G)).astype(BF16), st.astype(BF16))

    A = jnp.zeros((C, C), F32)
    b = sub
    while b < C:
        nb = C // b
        ends = [G[r * b + b - 1:r * b + b] for r in range(nb)]
        e_cur = jnp.concatenate([jnp.broadcast_to(e, (b, K)) for e in ends], axis=0)
        e_prev = jnp.concatenate([jnp.zeros((b, K), F32)]
                                 + [jnp.broadcast_to(e, (b, K)) for e in ends[:-1]], axis=0)
        qb = (q * jnp.exp(G - e_prev)).astype(BF16)
        kb = (k * jnp.exp(e_cur - G)).astype(BF16)
        sh = int(math.log2(b))
        in_level = (((row >> (sh + 1)) == (col >> (sh + 1)))
                    & (((row >> sh) & 1) == 1) & (((col >> sh) & 1) == 0))
        A = jnp.where(in_level, _nt_dot(qb, kb), A)
        b *= 2

    lane = _iota((sub, C), 1)
    trow = _iota((sub, 1), 0)
    blocks = []
    for i in range(C // sub):
        Gi = G[i * sub:(i + 1) * sub]
        qi = q[i * sub:(i + 1) * sub]
        Ai = A[i * sub:(i + 1) * sub]
        for s in range(sub):
            r = i * sub + s
            expo = jnp.where(trow >= s, Gi - G[r:r + 1], NEG_BIG)
            colv = jnp.sum(qi * k[r:r + 1] * jnp.exp(expo), axis=-1, keepdims=True)
            Ai = jnp.where(lane == r, colv, Ai)
        blocks.append(Ai)
    A = jnp.concatenate(blocks, axis=0) if len(blocks) > 1 else blocks[0]
    o = o + _dot(A.astype(BF16), v.astype(BF16))

    g_last = G[C - 1:C]
    k_dec = (k * jnp.exp(g_last - G)).astype(BF16)
    st_new = st * jnp.exp(g_last) + lax.dot_general(
        v.astype(BF16), k_dec, (((0,), (0,)), ((), ())), preferred_element_type=F32)
    return st_new, o


def _hgrn_prompt_kernel(lbraw_ref, gn_ref, mq_ref, mf_ref, mi_ref, mg_ref,
                        hq_ref, hf_ref, hi_ref, hg_ref, om_ref, op_ref, s_ref, *, layer, chunk):
    lb = _lower_bound(lbraw_ref, layer)
    gn = gn_ref[layer:layer + 1, :]
    n_meta = mq_ref.shape[0]
    K = hq_ref.shape[1]

    st, o = _gla_chunk(jnp.zeros((K, K), F32), mq_ref[...], mf_ref[...], mi_ref[...], lb,
                       valid_rows=n_meta, sub=min(16, n_meta))
    om_ref[...] = (_head_norm(o, gn) * _silu(mg_ref[...])).astype(om_ref.dtype)

    def body(ci, st):
        rows = pl.ds(pl.multiple_of(ci * chunk, chunk), chunk)
        st, o = _gla_chunk(st, hq_ref[rows, :], hf_ref[rows, :], hi_ref[rows, :], lb,
                           valid_rows=chunk, sub=16)
        op_ref[rows, :] = (_head_norm(o, gn) * _silu(hg_ref[rows, :])).astype(op_ref.dtype)
        return st

    st = lax.fori_loop(0, hq_ref.shape[0] // chunk, body, st)
    s_ref[...] = st.T


def _hgrn_prompt(z, lb_raw, gnorm, layer, lay):
    B, SEQ, H, NM, RP, RM = lay["B"], lay["SEQ"], lay["HG_HEADS"], lay["N_META"], lay["RP"], lay["RM"]
    chunk = _divisor_tile(SEQ, 128, 16)
    nl = lb_raw.shape[0]

    def col(k):
        return lambda b, h: (b, k * H + h)

    def mcol(k):
        return lambda b, h: (RP // NM + b, k * H + h)

    prompt = [pl.BlockSpec((SEQ, LANES), col(k)) for k in range(4)]
    meta = [pl.BlockSpec((NM, LANES), mcol(k)) for k in range(4)]
    vmem = 16 * SEQ * LANES * 4 + 8 * 1024 * 1024
    return pl.pallas_call(
        functools.partial(_hgrn_prompt_kernel, layer=layer, chunk=chunk),
        grid=(B, H),
        in_specs=[pl.BlockSpec((nl, LANES), lambda b, h: (0, h)),
                  pl.BlockSpec((nl, LANES), lambda b, h: (0, 0))] + meta + prompt,
        out_specs=[pl.BlockSpec((NM, LANES), lambda b, h: (b, h)),
                   pl.BlockSpec((SEQ, LANES), lambda b, h: (b, h)),
                   pl.BlockSpec((None, None, LANES, LANES), lambda b, h: (b, h, 0, 0))],
        out_shape=[jax.ShapeDtypeStruct((RM, H * LANES), BF16),
                   jax.ShapeDtypeStruct((RP, H * LANES), BF16),
                   jax.ShapeDtypeStruct((B, H, LANES, LANES), F32)],
        compiler_params=_params(("parallel", "parallel"), vmem),
        name="hgrn_prompt",
    )(lb_raw, gnorm, z, z, z, z, z, z, z, z)


def _hgrn_sample_kernel(lbraw_ref, gn_ref, s0_ref, hq_ref, hf_ref, hi_ref, hg_ref, o_ref, s_ref,
                        *, layer, valid_rows):
    lb = _lower_bound(lbraw_ref, layer)
    gn = gn_ref[layer:layer + 1, :]
    st, o = _gla_chunk(s0_ref[...].T, hq_ref[...], hf_ref[...], hi_ref[...], lb,
                       valid_rows=valid_rows, sub=hq_ref.shape[0])
    o_ref[...] = _head_norm(o, gn) * _silu(hg_ref[...])
    s_ref[...] = st.T


def _hgrn_sample(z, state, lb_raw, gnorm, layer, lay):
    DB, H, RS = lay["DB"], lay["HG_HEADS"], lay["RS"]
    base = (lay["RP"] + lay["RM"]) // SAMPLE_ROWS
    nl = lb_raw.shape[0]
    rows = [pl.BlockSpec((SAMPLE_ROWS, LANES), (lambda k: lambda b, h: (base + b, k * H + h))(k))
            for k in range(4)]
    return pl.pallas_call(
        functools.partial(_hgrn_sample_kernel, layer=layer, valid_rows=lay["DEC_SEQ"]),
        grid=(DB, H),
        in_specs=[pl.BlockSpec((nl, LANES), lambda b, h: (0, h)),
                  pl.BlockSpec((nl, LANES), lambda b, h: (0, 0)),
                  pl.BlockSpec((None, None, None, LANES, LANES), lambda b, h: (layer, b, h, 0, 0))] + rows,
        out_specs=[pl.BlockSpec((SAMPLE_ROWS, LANES), lambda b, h: (b, h)),
                   pl.BlockSpec((None, None, LANES, LANES), lambda b, h: (b, h, 0, 0))],
        out_shape=[jax.ShapeDtypeStruct((RS, H * LANES), F32),
                   jax.ShapeDtypeStruct((DB, H, LANES, LANES), F32)],
        compiler_params=_params(("parallel", "parallel"), 16 * 1024 * 1024),
        name="hgrn_sample",
    )(lb_raw, gnorm, state, z, z, z, z)


def _diff_lambda(lam_ref, layer):
    lp = lam_ref[...]
    a = jnp.sum(lp[0:1] * lp[1:2], axis=-1, keepdims=True)
    b = jnp.sum(lp[2:3] * lp[3:4], axis=-1, keepdims=True)
    lam_init = 0.8 - 0.6 * math.exp(-0.3 * layer)
    return jnp.exp(a) - jnp.exp(b) + lam_init, lam_init


def _softmax_step(carry, s, v):
    mx, l, acc = carry
    m_new = jnp.maximum(mx, jnp.max(s, axis=-1, keepdims=True))
    alpha = jnp.exp(mx - m_new)
    p = jnp.exp(s - m_new)
    return (m_new, alpha * l + jnp.sum(p, axis=-1, keepdims=True),
            alpha * acc + _dot(p.astype(BF16), v))


def _diff_prompt_kernel(lam_ref, g_ref, q_ref, k_ref, v_ref, km_ref, vm_ref, o_ref, *, layer, tk):
    qi = pl.program_id(2)
    tq, width = q_ref.shape
    d = width // 2
    scale = d ** -0.5
    lam, lam_init = _diff_lambda(lam_ref, layer)
    qpos = qi * tq + _iota((tq, 1), 0)
    n_chunks = (qi + 1) * (tq // tk)
    vm = vm_ref[...].astype(BF16)
    outs = []
    for m in range(2):
        cols = slice(m * d, (m + 1) * d)
        qm = q_ref[:, cols].astype(BF16)
        s = _nt_dot(qm, km_ref[:, cols].astype(BF16)) * scale
        mx = jnp.max(s, axis=-1, keepdims=True)
        p = jnp.exp(s - mx)
        carry = (mx, jnp.sum(p, axis=-1, keepdims=True), _dot(p.astype(BF16), vm))

        def body(j, carry):
            rows = pl.ds(pl.multiple_of(j * tk, tk), tk)
            s = _nt_dot(qm, k_ref[rows, cols].astype(BF16)) * scale
            kpos = j * tk + _iota((1, tk), 1)
            s = jnp.where(kpos <= qpos, s, NEG_BIG)
            return _softmax_step(carry, s, v_ref[rows, :].astype(BF16))

        _, l, acc = lax.fori_loop(0, n_chunks, body, carry)
        outs.append(acc / l)
    o = outs[0] - lam * outs[1]
    o_ref[...] = (_head_norm(o, g_ref[layer:layer + 1, :]) * (1.0 - lam_init)).astype(o_ref.dtype)


def _diff_prompt(dqk, z, lam, gain, layer, lay):
    B, SEQ, H, NM, RP = lay["B"], lay["SEQ"], lay["DIFF_HEADS"], lay["N_META"], lay["RP"]
    tq = _divisor_tile(SEQ, 256, 16)
    tk = _divisor_tile(tq, 256, 16)
    nq = SEQ // tq
    vcol = lay["DV_OFF"] // LANES
    nl = gain.shape[0]
    vmem = 8 * SEQ * LANES * 4 + 8 * 1024 * 1024
    return pl.pallas_call(
        functools.partial(_diff_prompt_kernel, layer=layer, tk=tk),
        grid=(B, H, nq),
        in_specs=[pl.BlockSpec((None,) + lam.shape[1:], lambda b, h, i: (layer, 0, 0)),
                  pl.BlockSpec((nl, LANES), lambda b, h, i: (0, 0)),
                  pl.BlockSpec((tq, LANES), lambda b, h, i: (b * nq + i, h)),
                  pl.BlockSpec((SEQ, LANES), lambda b, h, i: (b, H + h)),
                  pl.BlockSpec((SEQ, LANES), lambda b, h, i: (b, vcol + h)),
                  pl.BlockSpec((NM, LANES), lambda b, h, i: (RP // NM + b, H + h)),
                  pl.BlockSpec((NM, LANES), lambda b, h, i: (RP // NM + b, vcol + h))],
        out_specs=pl.BlockSpec((tq, LANES), lambda b, h, i: (b * nq + i, h)),
        out_shape=jax.ShapeDtypeStruct((RP, H * LANES), BF16),
        compiler_params=_params(("parallel", "parallel", "parallel"), vmem),
        name="diff_prompt",
    )(lam, gain, dqk, dqk, z, dqk, z)


def _diff_small_kernel(*refs, layer, heads, valid_rows, n_pages):
    if n_pages:
        _, lam_ref, g_ref, qk_ref, v_ref, kc_ref, vc_ref, o_ref, m_ref, l_ref, acc_ref = refs
    else:
        lam_ref, g_ref, qk_ref, v_ref, o_ref, m_ref, l_ref, acc_ref = refs
    T = qk_ref.shape[0]
    d = qk_ref.shape[1] // (4 * heads)
    scale = d ** -0.5
    step = pl.program_id(1)

    def q_map(m):
        return qk_ref[:, m * d:(m + 1) * d].astype(BF16)

    @pl.when(step == 0)
    def _():
        qidx = _iota((T, 1), 0)
        kidx = _iota((1, T), 1)
        visible = (kidx <= qidx) & (kidx < valid_rows)
        for m in range(2 * heads):
            h = m // 2
            km = qk_ref[:, (2 * heads + m) * d:(2 * heads + m + 1) * d].astype(BF16)
            s = jnp.where(visible, _nt_dot(q_map(m), km) * scale, NEG_BIG)
            mx = jnp.max(s, axis=-1, keepdims=True)
            p = jnp.exp(s - mx)
            rows = slice(m * T, (m + 1) * T)
            m_ref[rows, :] = mx
            l_ref[rows, :] = jnp.sum(p, axis=-1, keepdims=True)
            acc_ref[rows, :] = _dot(p.astype(BF16), v_ref[:, h * LANES:(h + 1) * LANES].astype(BF16))

    if n_pages:
        @pl.when(step > 0)
        def _():
            s = jnp.concatenate(
                [_nt_dot(q_map(m), kc_ref[0, 0, :, m, :].astype(BF16)) for m in range(2 * heads)],
                axis=0) * scale
            mx = m_ref[...]
            m_new = jnp.maximum(mx, jnp.max(s, axis=-1, keepdims=True))
            alpha = jnp.exp(mx - m_new)
            p = jnp.exp(s - m_new)
            m_ref[...] = m_new
            l_ref[...] = alpha * l_ref[...] + jnp.sum(p, axis=-1, keepdims=True)
            pb = p.astype(BF16)
            for h in range(heads):
                rows = slice(2 * h * T, (2 * h + 2) * T)
                acc_ref[rows, :] = alpha[rows] * acc_ref[rows, :] + _dot(
                    pb[rows], vc_ref[0, 0, :, h, :].astype(BF16))

    @pl.when(step == n_pages)
    def _():
        lam, lam_init = _diff_lambda(lam_ref, layer)
        o = acc_ref[...] / l_ref[...]
        g = g_ref[layer:layer + 1, :]
        for h in range(heads):
            oh = o[2 * h * T:(2 * h + 1) * T] - lam * o[(2 * h + 1) * T:(2 * h + 2) * T]
            o_ref[:, h * LANES:(h + 1) * LANES] = _head_norm(oh, g) * (1.0 - lam_init)


def _diff_small(dqk, z, lam, gain, layer, lay, *, row_block, rows, n_seq, valid_rows, cache=None):
    H = lay["DIFF_HEADS"]
    n_pages = lay["N_PAGES"] if cache is not None else 0
    nl = gain.shape[0]
    vcol = lay["DV_OFF"] // (H * LANES)

    def ix(f):
        return (lambda b, p, pt: f(b, p, pt)) if n_pages else (lambda b, p: f(b, p, None))

    in_specs = [pl.BlockSpec((None,) + lam.shape[1:], ix(lambda b, p, pt: (layer, 0, 0))),
                pl.BlockSpec((nl, LANES), ix(lambda b, p, pt: (0, 0))),
                pl.BlockSpec((rows, 2 * H * LANES), ix(lambda b, p, pt: (row_block + b, 0))),
                pl.BlockSpec((rows, H * LANES), ix(lambda b, p, pt: (row_block + b, vcol)))]
    args = [lam, gain, dqk, z]
    if n_pages:
        kc, vc, page_table = cache

        def page(b, p, pt):
            return (layer, pt[b, jnp.maximum(p, 1) - 1], 0, 0, 0)

        in_specs += [pl.BlockSpec((1, 1) + kc.shape[2:], page), pl.BlockSpec((1, 1) + vc.shape[2:], page)]
        args += [kc, vc]
    n_rows = 2 * H * rows
    kernel = functools.partial(_diff_small_kernel, layer=layer, heads=H, valid_rows=valid_rows,
                               n_pages=n_pages)
    out_spec = pl.BlockSpec((rows, H * LANES), ix(lambda b, p, pt: (b, 0)))
    scratch = [pltpu.VMEM((n_rows, 1), F32), pltpu.VMEM((n_rows, 1), F32), pltpu.VMEM((n_rows, LANES), F32)]
    out_shape = jax.ShapeDtypeStruct((n_seq * rows, H * LANES), F32)
    params = _params(("parallel", "arbitrary"), 32 * 1024 * 1024)
    if n_pages:
        return pl.pallas_call(
            kernel,
            grid_spec=pltpu.PrefetchScalarGridSpec(
                num_scalar_prefetch=1, grid=(n_seq, n_pages + 1), in_specs=in_specs,
                out_specs=out_spec, scratch_shapes=scratch),
            out_shape=out_shape, compiler_params=params, name="diff_sample",
        )(page_table, *args)
    return pl.pallas_call(
        kernel, grid=(n_seq, 1), in_specs=in_specs, out_specs=out_spec, scratch_shapes=scratch,
        out_shape=out_shape, compiler_params=params, name="diff_meta",
    )(*args)


def _sb_block(q, k, v, scale, carry, acc, visible, triu):
    z = _nt_dot(q, k) * scale
    lp = jnp.log1p(jnp.exp(-jnp.abs(z)))
    log_beta = jnp.minimum(z, 0.0) - lp
    log_1mb = -jnp.maximum(z, 0.0) - lp
    if visible is not None:
        log_1mb = jnp.where(visible, log_1mb, 0.0)
    tail = None
    for part in _split_bf16(log_1mb, 2):
        t = _dot(part, triu)
        tail = t if tail is None else tail + t
    w = log_beta + tail + carry
    if visible is not None:
        w = jnp.where(visible, w, NEG_BIG)
    acc = acc + _dot(jnp.exp(w).astype(BF16), v)
    return carry + jnp.sum(log_1mb, axis=-1, keepdims=True), acc


def _triu(n):
    return (_iota((n, n), 0) > _iota((n, n), 1)).astype(BF16)


def _sb_prompt_kernel(g_ref, q_ref, k_ref, v_ref, km_ref, vm_ref, o_ref, *, layer, tk):
    qi = pl.program_id(2)
    tq, d = q_ref.shape
    scale = d ** -0.5
    q = q_ref[...].astype(BF16)
    qpos = qi * tq + _iota((tq, 1), 0)
    n_chunks = (qi + 1) * (tq // tk)
    triu = _triu(tk)

    def body(i, state):
        j = n_chunks - 1 - i
        rows = pl.ds(pl.multiple_of(j * tk, tk), tk)
        visible = (j * tk + _iota((1, tk), 1)) < qpos
        return _sb_block(q, k_ref[rows, :].astype(BF16), v_ref[rows, :].astype(BF16), scale,
                         state[0], state[1], visible, triu)

    carry, acc = lax.fori_loop(0, n_chunks, body, (jnp.zeros((tq, 1), F32), jnp.zeros((tq, d), F32)))
    _, acc = _sb_block(q, km_ref[...].astype(BF16), vm_ref[...].astype(BF16), scale, carry, acc,
                       None, _triu(km_ref.shape[0]))
    o_ref[...] = _head_norm(acc, g_ref[layer:layer + 1, :]).astype(o_ref.dtype)


def _sb_prompt(z, gain, layer, lay):
    B, SEQ, H, NM, RP = lay["B"], lay["SEQ"], lay["SB_HEADS"], lay["N_META"], lay["RP"]
    tq = _divisor_tile(SEQ, 256, 16)
    tk = _divisor_tile(tq, 128, 16)
    nq = SEQ // tq
    qc, kc, vc = (lay["SQ_OFF"] // LANES, lay["SK_OFF"] // LANES, lay["SV_OFF"] // LANES)
    nl = gain.shape[0]
    vmem = 8 * SEQ * LANES * 4 + 8 * 1024 * 1024
    return pl.pallas_call(
        functools.partial(_sb_prompt_kernel, layer=layer, tk=tk),
        grid=(B, H, nq),
        in_specs=[pl.BlockSpec((nl, LANES), lambda b, h, i: (0, 0)),
                  pl.BlockSpec((tq, LANES), lambda b, h, i: (b * nq + i, qc + h)),
                  pl.BlockSpec((SEQ, LANES), lambda b, h, i: (b, kc + h)),
                  pl.BlockSpec((SEQ, LANES), lambda b, h, i: (b, vc + h)),
                  pl.BlockSpec((NM, LANES), lambda b, h, i: (RP // NM + b, kc + h)),
                  pl.BlockSpec((NM, LANES), lambda b, h, i: (RP // NM + b, vc + h))],
        out_specs=pl.BlockSpec((tq, LANES), lambda b, h, i: (b * nq + i, h)),
        out_shape=jax.ShapeDtypeStruct((RP, H * LANES), BF16),
        compiler_params=_params(("parallel", "parallel", "parallel"), vmem),
        name="sb_prompt",
    )(gain, z, z, z, z, z)


def _sb_small_kernel(*refs, layer, heads, pieces, valid_rows, n_pages):
    if n_pages:
        refs = refs[1:]
    g_ref = refs[0]
    q_ref, k_ref, v_ref = (refs[1 + i * pieces:1 + (i + 1) * pieces] for i in range(3))
    rest = refs[1 + 3 * pieces:]
    if n_pages:
        kc_ref, vc_ref, o_ref, carry_ref, acc_ref = rest
    else:
        o_ref, carry_ref, acc_ref = rest
    T = o_ref.shape[0]
    d = LANES
    scale = d ** -0.5
    step = pl.program_id(1)
    per_piece = heads // pieces

    def head(piece_refs, h):
        c = (h % per_piece) * d
        return piece_refs[h // per_piece][:, c:c + d].astype(BF16)

    @pl.when(step == 0)
    def _():
        qidx = _iota((T, 1), 0)
        kidx = _iota((1, T), 1)
        visible = (kidx < qidx) & (kidx < valid_rows)
        triu = _triu(T)
        for h in range(heads):
            carry, acc = _sb_block(head(q_ref, h), head(k_ref, h), head(v_ref, h), scale,
                                   jnp.zeros((T, 1), F32), jnp.zeros((T, d), F32), visible, triu)
            carry_ref[h * T:(h + 1) * T, :] = carry
            acc_ref[h * T:(h + 1) * T, :] = acc

    if n_pages:
        @pl.when(step > 0)
        def _():
            page = kc_ref.shape[2]
            z = jnp.concatenate(
                [_nt_dot(head(q_ref, h), kc_ref[0, 0, :, h, :].astype(BF16)) for h in range(heads)],
                axis=0) * scale
            lp = jnp.log1p(jnp.exp(-jnp.abs(z)))
            log_beta = jnp.minimum(z, 0.0) - lp
            log_1mb = -jnp.maximum(z, 0.0) - lp
            triu = _triu(page)
            tail = None
            for part in _split_bf16(log_1mb, 2):
                t = _dot(part, triu)
                tail = t if tail is None else tail + t
            a = jnp.exp(log_beta + tail + carry_ref[...]).astype(BF16)
            carry_ref[...] += jnp.sum(log_1mb, axis=-1, keepdims=True)
            for h in range(heads):
                rows = slice(h * T, (h + 1) * T)
                acc_ref[rows, :] += _dot(a[rows], vc_ref[0, 0, :, h, :].astype(BF16))

    @pl.when(step == n_pages)
    def _():
        o = _head_norm(acc_ref[...], g_ref[layer:layer + 1, :])
        for h in range(heads):
            o_ref[:, h * d:(h + 1) * d] = o[h * T:(h + 1) * T]


def _sb_small(z, gain, layer, lay, *, row_block, rows, n_seq, valid_rows, cache=None):
    H = lay["SB_HEADS"]
    n_pages = lay["N_PAGES"] if cache is not None else 0
    nl = gain.shape[0]
    width = H * LANES
    cw = math.gcd(lay["SQ_OFF"], width)
    pieces = width // cw

    def ix(f):
        return (lambda b, p, pt: f(b, p, pt)) if n_pages else (lambda b, p: f(b, p, None))

    in_specs = [pl.BlockSpec((nl, LANES), ix(lambda b, p, pt: (0, 0)))]
    in_specs += [pl.BlockSpec((rows, cw), ix((lambda c: lambda b, p, pt: (row_block + b, c))(off // cw + i)))
                 for off in (lay["SQ_OFF"], lay["SK_OFF"], lay["SV_OFF"]) for i in range(pieces)]
    args = [gain] + [z] * (3 * pieces)
    if n_pages:
        kc, vc, page_table = cache

        def page(b, p, pt):
            return (layer, pt[b, n_pages - jnp.maximum(p, 1)], 0, 0, 0)

        in_specs += [pl.BlockSpec((1, 1) + kc.shape[2:], page), pl.BlockSpec((1, 1) + vc.shape[2:], page)]
        args += [kc, vc]
    kernel = functools.partial(_sb_small_kernel, layer=layer, heads=H, pieces=pieces,
                               valid_rows=valid_rows, n_pages=n_pages)
    out_spec = pl.BlockSpec((rows, width), ix(lambda b, p, pt: (b, 0)))
    scratch = [pltpu.VMEM((H * rows, 1), F32), pltpu.VMEM((H * rows, LANES), F32)]
    out_shape = jax.ShapeDtypeStruct((n_seq * rows, width), F32)
    params = _params(("parallel", "arbitrary"), 32 * 1024 * 1024)
    if n_pages:
        return pl.pallas_call(
            kernel,
            grid_spec=pltpu.PrefetchScalarGridSpec(
                num_scalar_prefetch=1, grid=(n_seq, n_pages + 1), in_specs=in_specs,
                out_specs=out_spec, scratch_shapes=scratch),
            out_shape=out_shape, compiler_params=params, name="sb_sample",
        )(page_table, *args)
    return pl.pallas_call(
        kernel, grid=(n_seq, 1), in_specs=in_specs, out_specs=out_spec, scratch_shapes=scratch,
        out_shape=out_shape, compiler_params=params, name="sb_meta",
    )(*args)


def _rope_tables(pos, d_map, rope_dim):
    half = rope_dim // 2
    inv = ROPE_THETA ** (-jnp.arange(half, dtype=F32) / half)
    ang = pos.astype(F32)[:, None] * inv[None, :]
    cos, sin = jnp.cos(ang), jnp.sin(ang)
    rows = pos.shape[0]
    pad = jnp.zeros((rows, d_map - rope_dim), F32)
    zero = jnp.zeros((rows, half), F32)
    c = jnp.concatenate([cos, cos, pad + 1.0], axis=1)
    s1 = jnp.concatenate([-sin, zero, pad], axis=1)
    s2 = jnp.concatenate([zero, sin, pad], axis=1)
    rep = LANES // d_map
    return tuple(jnp.tile(t, (1, rep)) for t in (c, s1, s2))


def kernel(x_prompt, x_sample, cache_diff_k, cache_diff_v, cache_sb_k, cache_sb_v, state_hgrn,
           page_table, meta_tokens, norm_mix, norm_ffn, norm_final, w_in, hg_lower_bounds, hg_norm,
           diff_lambda, diff_subln, sb_norm, w_out, w_gate_up, w_down):
    B, SEQ, D = x_prompt.shape
    DB, DEC_SEQ, _ = x_sample.shape
    DEPTH = w_in.shape[0]
    NM = meta_tokens.shape[0]
    HGH = state_hgrn.shape[2]
    DH = cache_diff_v.shape[3]
    DQK = cache_diff_k.shape[4]
    SH = cache_sb_k.shape[3]
    PAGE = cache_sb_k.shape[2]
    NP = page_table.shape[1]
    assert DEC_SEQ <= SAMPLE_ROWS and NM % 16 == 0 and SEQ % NM == 0 and 2 * DQK == LANES
    assert state_hgrn.shape[3:] == (LANES, LANES) and HGH == DH
    HGW, DW, SW = HGH * LANES, DH * LANES, SH * LANES
    RP, RM, RS = B * SEQ, B * NM, DB * SAMPLE_ROWS
    lay = dict(B=B, SEQ=SEQ, DB=DB, DEC_SEQ=DEC_SEQ, N_META=NM, HG_HEADS=HGH, DIFF_HEADS=DH, SB_HEADS=SH,
               N_PAGES=NP, RP=RP, RM=RM, RS=RS,
               DV_OFF=4 * HGW + 2 * DW, SQ_OFF=4 * HGW + 3 * DW, SK_OFF=4 * HGW + 3 * DW + SW,
               SV_OFF=4 * HGW + 3 * DW + 2 * SW)

    sample = jnp.pad(x_sample, ((0, 0), (0, SAMPLE_ROWS - DEC_SEQ), (0, 0)))
    x = jnp.concatenate([x_prompt.reshape(RP, D),
                         jnp.broadcast_to(meta_tokens[None], (B, NM, D)).reshape(RM, D),
                         sample.reshape(RS, D)], axis=0)
    pos = jnp.concatenate([jnp.tile(NM + jnp.arange(SEQ, dtype=jnp.int32), B),
                           jnp.tile(jnp.arange(NM, dtype=jnp.int32), B),
                           jnp.tile(NP * PAGE + jnp.arange(SAMPLE_ROWS, dtype=jnp.int32), DB)])
    tables = _rope_tables(pos, DQK, DQK // 4)

    w_in_b, w_out_b, w_gu_b, w_dn_b = (w.astype(BF16) for w in (w_in, w_out, w_gate_up, w_down))
    g_mix = norm_mix.reshape(DEPTH, 1, D)
    g_ffn = norm_ffn.reshape(DEPTH, 1, D)
    g_fin = norm_final.reshape(1, 1, D)
    meta_blk = RP // NM
    samp_blk = (RP + RM) // SAMPLE_ROWS
    small_meta = dict(row_block=meta_blk, rows=NM, n_seq=B, valid_rows=NM)
    small_samp = dict(row_block=samp_blk, rows=SAMPLE_ROWS, n_seq=DB, valid_rows=DEC_SEQ)

    outs = {k: [] for k in ("pk_d", "pv_d", "pk_s", "pv_s", "ps_h", "sk_d", "sv_d", "sk_s", "sv_s", "ss_h")}

    def cache_rows(a, off, width, heads):
        a = a[:, off:off + width]
        p = jnp.concatenate([a[RP:RP + RM].reshape(B, NM, heads, -1), a[:RP].reshape(B, SEQ, heads, -1)], axis=1)
        s = a[RP + RM:].reshape(DB, SAMPLE_ROWS, heads, -1)[:, :DEC_SEQ]
        return p, s

    for l in range(DEPTH):
        hn = _rmsnorm(x, g_mix, l, BF16)
        z = _matmul(hn, w_in_b, l)
        dqk = _rope(z, 2, 2 * DW, tables)

        hg_m, hg_p, s_p = _hgrn_prompt(z, hg_lower_bounds, hg_norm, l, lay)
        hg_s, s_s = _hgrn_sample(z, state_hgrn, hg_lower_bounds, hg_norm, l, lay)
        df_p = _diff_prompt(dqk, z, diff_lambda, diff_subln, l, lay)
        df_m = _diff_small(dqk, z, diff_lambda, diff_subln, l, lay, **small_meta)
        df_s = _diff_small(dqk, z, diff_lambda, diff_subln, l, lay, **small_samp,
                           cache=(cache_diff_k, cache_diff_v, page_table))
        sb_p = _sb_prompt(z, sb_norm, l, lay)
        sb_m = _sb_small(z, sb_norm, l, lay, **small_meta)
        sb_s = _sb_small(z, sb_norm, l, lay, **small_samp, cache=(cache_sb_k, cache_sb_v, page_table))

        mix = jnp.concatenate([
            jnp.concatenate([hg_p, df_p, sb_p], axis=1),
            jnp.concatenate([hg_m, df_m.astype(BF16), sb_m.astype(BF16)], axis=1),
            jnp.concatenate([hg_s.astype(BF16), df_s.astype(BF16), sb_s.astype(BF16)], axis=1)], axis=0)
        x = _matmul(mix, w_out_b, l, res=x)
        h = _ffn_up(_rmsnorm(x, g_ffn, l, BF16), w_gu_b, l)
        x = _matmul(h, w_dn_b, l, res=x)

        for name_p, name_s, (p, s) in (
                ("pk_d", "sk_d", cache_rows(dqk, DW, DW, 2 * DH)),
                ("pv_d", "sv_d", cache_rows(z, lay["DV_OFF"], DW, DH)),
                ("pk_s", "sk_s", cache_rows(z, lay["SK_OFF"], SW, SH)),
                ("pv_s", "sv_s", cache_rows(z, lay["SV_OFF"], SW, SH))):
            outs[name_p].append(p)
            outs[name_s].append(s)
        outs["ps_h"].append(s_p)
        outs["ss_h"].append(s_s)

    y = _rmsnorm(x, g_fin, 0, F32)
    y_prompt = y[:RP].reshape(B, SEQ, D)
    y_sample = y[RP + RM:].reshape(DB, SAMPLE_ROWS, D)[:, :DEC_SEQ]
    st = lambda k: jnp.stack(outs[k], axis=0)
    return (y_prompt, y_sample, st("pk_d"), st("pv_d"), st("pk_s"), st("pv_s"), st("ps_h"),
            st("sk_d"), st("sv_d"), st("sk_s"), st("sv_s"), st("ss_h"))
```

```python
import functools
import math

import jax
import jax.numpy as jnp
from jax import lax
from jax.experimental import pallas as pl
from jax.experimental.pallas import tpu as pltpu

F32 = jnp.float32
BF16 = jnp.bfloat16
EPS = 1e-6
NEG_BIG = -1e30
ROPE_THETA = 500000.0
LANES = 128
SAMPLE_ROWS = 8
V7X_VMEM_BYTES = 64 * 1024 * 1024
VMEM_CAP = V7X_VMEM_BYTES - 8 * 1024 * 1024


def _params(semantics, vmem_bytes):
    limit = int(min(max(vmem_bytes, 16 * 1024 * 1024), VMEM_CAP))
    return pltpu.CompilerParams(dimension_semantics=semantics, vmem_limit_bytes=limit)


def _divisor_tile(n, target, multiple):
    best = None
    for t in range(multiple, min(n, target) + 1, multiple):
        if n % t == 0:
            best = t
    return best if best is not None else n


def _pages_per_step(n_pages):
    return 2 if n_pages > 0 and n_pages % 2 == 0 else 1


def _nt_dot(a, b):
    return lax.dot_general(a, b, (((1,), (1,)), ((), ())), preferred_element_type=F32)


def _dot(a, b):
    return jnp.dot(a, b, preferred_element_type=F32)


def _iota(shape, dim):
    return lax.broadcasted_iota(jnp.int32, shape, dim)


def _split_bf16(x, parts):
    out = []
    r = x
    for i in range(parts):
        p = r.astype(BF16)
        out.append(p)
        if i + 1 < parts:
            r = r - p.astype(F32)
    return out


def _sigmoid_pair(z):
    e = jnp.exp(-jnp.abs(z))
    r = 1.0 / (1.0 + e)
    er = e * r
    pos = z >= 0
    return jnp.where(pos, r, er), jnp.where(pos, er, r)


def _silu(z):
    return z * _sigmoid_pair(z)[0]


def _head_norm(o, g):
    return o * lax.rsqrt(jnp.mean(o * o, axis=-1, keepdims=True) + EPS) * g


def _rmsnorm_kernel(x_ref, g_ref, o_ref):
    x = x_ref[...]
    y = x * lax.rsqrt(jnp.mean(x * x, axis=-1, keepdims=True) + EPS)
    o_ref[...] = (y * g_ref[...]).astype(o_ref.dtype)


def _rmsnorm(x, gains, layer, out_dtype):
    M, D = x.shape
    tr = _divisor_tile(M, 256, 16)
    vmem = 2 * tr * D * (4 + jnp.dtype(out_dtype).itemsize) + 3 * tr * D * 4
    return pl.pallas_call(
        _rmsnorm_kernel,
        grid=(M // tr,),
        in_specs=[pl.BlockSpec((tr, D), lambda i: (i, 0)),
                  pl.BlockSpec((None, 1, D), lambda i: (layer, 0, 0))],
        out_specs=pl.BlockSpec((tr, D), lambda i: (i, 0)),
        out_shape=jax.ShapeDtypeStruct((M, D), out_dtype),
        compiler_params=_params(("parallel",), vmem),
        name="rmsnorm",
    )(x, gains)


def _rope_kernel(x_ref, c_ref, s1_ref, s2_ref, o_ref):
    c = c_ref[...]
    s1 = s1_ref[...]
    s2 = s2_ref[...]
    for j in range(x_ref.shape[1] // LANES):
        x = x_ref[:, j * LANES:(j + 1) * LANES]
        up = pltpu.roll(x, LANES - 8, 1)
        dn = pltpu.roll(x, 8, 1)
        o_ref[:, j * LANES:(j + 1) * LANES] = x * c + up * s1 + dn * s2


def _rope(z, col_block, width, tables):
    M = z.shape[0]
    tr = _divisor_tile(M, 512, 8)
    tab = pl.BlockSpec((tr, LANES), lambda i: (i, 0))
    vmem = 4 * tr * width * 4 + 6 * tr * LANES * 4 + 4 * tr * LANES * 4
    return pl.pallas_call(
        _rope_kernel,
        grid=(M // tr,),
        in_specs=[pl.BlockSpec((tr, width), lambda i: (i, col_block)), tab, tab, tab],
        out_specs=pl.BlockSpec((tr, width), lambda i: (i, 0)),
        out_shape=jax.ShapeDtypeStruct((M, width), F32),
        compiler_params=_params(("parallel",), vmem),
        name="rope",
    )(z, *tables)


def _matmul_kernel(a_ref, w_ref, *rest, nk, has_res):
    o_ref = rest[-1]
    p = _dot(a_ref[...], w_ref[...].astype(BF16))
    if has_res:
        first = p + rest[0][...]
    else:
        first = p
    if nk == 1:
        o_ref[...] = first
    else:
        k = pl.program_id(2)

        @pl.when(k == 0)
        def _():
            o_ref[...] = first

        @pl.when(k > 0)
        def _():
            o_ref[...] += p


def _matmul_tiles(M, K, N):
    bk = K if K <= 4096 else _divisor_tile(K, 5632, LANES)
    bm = _divisor_tile(M, 1664 if bk == K else 832, 16)
    bn = _divisor_tile(N, 512 if bk == K else 256, LANES)
    return bm, bk, bn


def _matmul(a, w, layer, res=None):
    M, K = a.shape
    N = w.shape[2]
    bm, bk, bn = _matmul_tiles(M, K, N)
    nk = K // bk
    a_bufs = 1 if nk == 1 else 2
    a_kw = dict(pipeline_mode=pl.Buffered(1)) if nk == 1 else {}
    in_specs = [pl.BlockSpec((bm, bk), lambda i, j, k: (i, k), **a_kw),
                pl.BlockSpec((None, bk, bn), lambda i, j, k: (layer, k, j))]
    args = [a, w]
    if res is not None:
        in_specs.append(pl.BlockSpec((bm, bn), lambda i, j, k: (i, j)))
        args.append(res)
    vmem = (a_bufs * bm * bk * 2 + 2 * bk * bn * w.dtype.itemsize + bk * bn * 2
            + (4 if res is not None else 2) * bm * bn * 4 + 2 * bm * bn * 4)
    return pl.pallas_call(
        functools.partial(_matmul_kernel, nk=nk, has_res=res is not None),
        grid=(M // bm, N // bn, nk),
        in_specs=in_specs,
        out_specs=pl.BlockSpec((bm, bn), lambda i, j, k: (i, j)),
        out_shape=jax.ShapeDtypeStruct((M, N), F32),
        compiler_params=_params(("parallel", "parallel", "arbitrary"), vmem),
        name="matmul",
    )(*args)


def _ffn_up_kernel(a_ref, wg_ref, wu_ref, o_ref):
    a = a_ref[...]
    g = _dot(a, wg_ref[...].astype(BF16))
    u = _dot(a, wu_ref[...].astype(BF16))
    o_ref[...] = (_silu(g) * u).astype(o_ref.dtype)


def _ffn_up(a, w_gu, layer):
    M, K = a.shape
    F = w_gu.shape[2] // 2
    bm = _divisor_tile(M, 1664, 16)
    bn = _divisor_tile(F, 256, LANES)
    nf = F // bn
    vmem = (bm * K * 2 + 4 * K * bn * w_gu.dtype.itemsize + 2 * K * bn * 2 + 2 * bm * bn * 2
            + 4 * bm * bn * 4)
    return pl.pallas_call(
        _ffn_up_kernel,
        grid=(M // bm, nf),
        in_specs=[pl.BlockSpec((bm, K), lambda i, j: (i, 0), pipeline_mode=pl.Buffered(1)),
                  pl.BlockSpec((None, K, bn), lambda i, j: (layer, 0, j)),
                  pl.BlockSpec((None, K, bn), lambda i, j: (layer, 0, nf + j))],
        out_specs=pl.BlockSpec((bm, bn), lambda i, j: (i, j)),
        out_shape=jax.ShapeDtypeStruct((M, F), BF16),
        compiler_params=_params(("parallel", "parallel"), vmem),
        name="ffn_up",
    )(a, w_gu, w_gu)


def _lower_bound(lbraw_ref, layer):
    raw = lbraw_ref[...]
    e = jnp.exp(raw - jnp.max(raw, axis=0, keepdims=True))
    total = jnp.sum(e, axis=0, keepdims=True)
    if layer == 0:
        return jnp.zeros_like(total)
    return jnp.sum(e[1:layer + 1], axis=0, keepdims=True) / total


def _gla_chunk(st, hq, hf, hi, lb, *, valid_rows, sub):
    C, K = hq.shape
    sig, sig_neg = _sigmoid_pair(hf)
    q = _silu(hq) * (K ** -0.5)
    gl = jnp.log(lb + (1.0 - lb) * sig)
    k = (1.0 - lb) * sig_neg
    v = hi
    if valid_rows < C:
        ok = _iota((C, 1), 0) < valid_rows
        gl = jnp.where(ok, gl, 0.0)
        k = jnp.where(ok, k, 0.0)

    row = _iota((C, C), 0)
    col = _iota((C, C), 1)
    tril = (col <= row).astype(BF16)
    G = None
    for part in _split_bf16(gl, 3):
        t = _dot(tril, part)
        G = t if G is None else G + t

    o = _nt_dot((q * jnp.exp(G)).astype(BF16), st.astype(BF16))

    A = jnp.zeros((C, C), F32)
    b = sub
    while b < C:
        nb = C // b
        ends = [G[r * b + b - 1:r * b + b] for r in range(nb)]
        e_cur = jnp.concatenate([jnp.broadcast_to(e, (b, K)) for e in ends], axis=0)
        e_prev = jnp.concatenate([jnp.zeros((b, K), F32)]
                                 + [jnp.broadcast_to(e, (b, K)) for e in ends[:-1]], axis=0)
        qb = (q * jnp.exp(G - e_prev)).astype(BF16)
        kb = (k * jnp.exp(e_cur - G)).astype(BF16)
        sh = int(math.log2(b))
        in_level = (((row >> (sh + 1)) == (col >> (sh + 1)))
                    & (((row >> sh) & 1) == 1) & (((col >> sh) & 1) == 0))
        A = jnp.where(in_level, _nt_dot(qb, kb), A)
        b *= 2

    lane = _iota((sub, C), 1)
    trow = _iota((sub, 1), 0)
    blocks = []
    for i in range(C // sub):
        Gi = G[i * sub:(i + 1) * sub]
        qi = q[i * sub:(i + 1) * sub]
        Ai = A[i * sub:(i + 1) * sub]
        for s in range(sub):
            r = i * sub + s
            expo = jnp.where(trow >= s, Gi - G[r:r + 1], NEG_BIG)
            colv = jnp.sum(qi * k[r:r + 1] * jnp.exp(expo), axis=-1, keepdims=True)
            Ai = jnp.where(lane == r, colv, Ai)
        blocks.append(Ai)
    A = jnp.concatenate(blocks, axis=0) if len(blocks) > 1 else blocks[0]
    o = o + _dot(A.astype(BF16), v.astype(BF16))

    g_last = G[C - 1:C]
    k_dec = (k * jnp.exp(g_last - G)).astype(BF16)
    st_new = st * jnp.exp(g_last) + lax.dot_general(
        v.astype(BF16), k_dec, (((0,), (0,)), ((), ())), preferred_element_type=F32)
    return st_new, o


def _hgrn_prompt_kernel(lbraw_ref, gn_ref, mq_ref, mf_ref, mi_ref, mg_ref,
                        hq_ref, hf_ref, hi_ref, hg_ref, om_ref, op_ref, s_ref, *, layer, chunk):
    lb = _lower_bound(lbraw_ref, layer)
    gn = gn_ref[layer:layer + 1, :]
    n_meta = mq_ref.shape[0]
    K = hq_ref.shape[1]

    st, o = _gla_chunk(jnp.zeros((K, K), F32), mq_ref[...], mf_ref[...], mi_ref[...], lb,
                       valid_rows=n_meta, sub=min(16, n_meta))
    om_ref[...] = (_head_norm(o, gn) * _silu(mg_ref[...])).astype(om_ref.dtype)

    def body(ci, st):
        rows = pl.ds(pl.multiple_of(ci * chunk, chunk), chunk)
        st, o = _gla_chunk(st, hq_ref[rows, :], hf_ref[rows, :], hi_ref[rows, :], lb,
                           valid_rows=chunk, sub=16)
        op_ref[rows, :] = (_head_norm(o, gn) * _silu(hg_ref[rows, :])).astype(op_ref.dtype)
        return st

    st = lax.fori_loop(0, hq_ref.shape[0] // chunk, body, st)
    s_ref[...] = st.T


def _hgrn_prompt(z, lb_raw, gnorm, layer, lay):
    B, SEQ, H, NM, RP, RM = lay["B"], lay["SEQ"], lay["HG_HEADS"], lay["N_META"], lay["RP"], lay["RM"]
    chunk = _divisor_tile(SEQ, 128, 16)
    nl = lb_raw.shape[0]

    def col(k):
        return lambda b, h: (b, k * H + h)

    def mcol(k):
        return lambda b, h: (RP // NM + b, k * H + h)

    prompt = [pl.BlockSpec((SEQ, LANES), col(k)) for k in range(4)]
    meta = [pl.BlockSpec((NM, LANES), mcol(k)) for k in range(4)]
    vmem = 16 * SEQ * LANES * 4 + 8 * 1024 * 1024
    return pl.pallas_call(
        functools.partial(_hgrn_prompt_kernel, layer=layer, chunk=chunk),
        grid=(B, H),
        in_specs=[pl.BlockSpec((nl, LANES), lambda b, h: (0, h)),
                  pl.BlockSpec((nl, LANES), lambda b, h: (0, 0))] + meta + prompt,
        out_specs=[pl.BlockSpec((NM, LANES), lambda b, h: (b, h)),
                   pl.BlockSpec((SEQ, LANES), lambda b, h: (b, h)),
                   pl.BlockSpec((None, None, LANES, LANES), lambda b, h: (b, h, 0, 0))],
        out_shape=[jax.ShapeDtypeStruct((RM, H * LANES), BF16),
                   jax.ShapeDtypeStruct((RP, H * LANES), BF16),
                   jax.ShapeDtypeStruct((B, H, LANES, LANES), F32)],
        compiler_params=_params(("parallel", "parallel"), vmem),
        name="hgrn_prompt",
    )(lb_raw, gnorm, z, z, z, z, z, z, z, z)


def _hgrn_sample_kernel(lbraw_ref, gn_ref, s0_ref, hq_ref, hf_ref, hi_ref, hg_ref, o_ref, s_ref,
                        *, layer, valid_rows):
    lb = _lower_bound(lbraw_ref, layer)
    gn = gn_ref[layer:layer + 1, :]
    st, o = _gla_chunk(s0_ref[...].T, hq_ref[...], hf_ref[...], hi_ref[...], lb,
                       valid_rows=valid_rows, sub=hq_ref.shape[0])
    o_ref[...] = _head_norm(o, gn) * _silu(hg_ref[...])
    s_ref[...] = st.T


def _hgrn_sample(z, state, lb_raw, gnorm, layer, lay):
    DB, H, RS = lay["DB"], lay["HG_HEADS"], lay["RS"]
    base = (lay["RP"] + lay["RM"]) // SAMPLE_ROWS
    nl = lb_raw.shape[0]
    rows = [pl.BlockSpec((SAMPLE_ROWS, LANES), (lambda k: lambda b, h: (base + b, k * H + h))(k))
            for k in range(4)]
    return pl.pallas_call(
        functools.partial(_hgrn_sample_kernel, layer=layer, valid_rows=lay["DEC_SEQ"]),
        grid=(DB, H),
        in_specs=[pl.BlockSpec((nl, LANES), lambda b, h: (0, h)),
                  pl.BlockSpec((nl, LANES), lambda b, h: (0, 0)),
                  pl.BlockSpec((None, None, None, LANES, LANES), lambda b, h: (layer, b, h, 0, 0))] + rows,
        out_specs=[pl.BlockSpec((SAMPLE_ROWS, LANES), lambda b, h: (b, h)),
                   pl.BlockSpec((None, None, LANES, LANES), lambda b, h: (b, h, 0, 0))],
        out_shape=[jax.ShapeDtypeStruct((RS, H * LANES), F32),
                   jax.ShapeDtypeStruct((DB, H, LANES, LANES), F32)],
        compiler_params=_params(("parallel", "parallel"), 16 * 1024 * 1024),
        name="hgrn_sample",
    )(lb_raw, gnorm, state, z, z, z, z)


def _diff_lambda(lam_ref, layer):
    lp = lam_ref[...]
    a = jnp.sum(lp[0:1] * lp[1:2], axis=-1, keepdims=True)
    b = jnp.sum(lp[2:3] * lp[3:4], axis=-1, keepdims=True)
    lam_init = 0.8 - 0.6 * math.exp(-0.3 * layer)
    return jnp.exp(a) - jnp.exp(b) + lam_init, lam_init


def _softmax_step(carry, s, v):
    mx, l, acc = carry
    m_new = jnp.maximum(mx, jnp.max(s, axis=-1, keepdims=True))
    alpha = jnp.exp(mx - m_new)
    p = jnp.exp(s - m_new)
    return (m_new, alpha * l + jnp.sum(p, axis=-1, keepdims=True),
            alpha * acc + _dot(p.astype(BF16), v))


def _diff_prompt_kernel(lam_ref, g_ref, *refs, layer, hp):
    q_refs, k_refs, v_refs, km_refs, vm_refs = (refs[i * hp:(i + 1) * hp] for i in range(5))
    o_ref = refs[5 * hp]
    qi = pl.program_id(2)
    tq, width = q_refs[0].shape
    d = width // 2
    scale = d ** -0.5
    lam, lam_init = _diff_lambda(lam_ref, layer)
    g = g_ref[layer:layer + 1, :]
    qs = [[r[:, m * d:(m + 1) * d].astype(BF16) for m in range(2)] for r in q_refs]

    def scores(h, k):
        return jnp.concatenate(
            [_nt_dot(qs[h][m], k[:, m * d:(m + 1) * d].astype(BF16)) for m in range(2)], axis=0) * scale

    state = []
    for h in range(hp):
        s = scores(h, km_refs[h][...])
        mx = jnp.max(s, axis=-1, keepdims=True)
        p = jnp.exp(s - mx)
        state.append((mx, jnp.sum(p, axis=-1, keepdims=True),
                      _dot(p.astype(BF16), vm_refs[h][...].astype(BF16))))

    diag = pl.ds(pl.multiple_of(qi * tq, tq), tq)
    causal = _iota((1, tq), 1) <= _iota((tq, 1), 0)
    causal = jnp.concatenate([causal, causal], axis=0)
    for h in range(hp):
        s = jnp.where(causal, scores(h, k_refs[h][diag, :]), NEG_BIG)
        state[h] = _softmax_step(state[h], s, v_refs[h][diag, :].astype(BF16))

    def body(j, state):
        rows = pl.ds(pl.multiple_of(j * tq, tq), tq)
        return tuple(_softmax_step(state[h], scores(h, k_refs[h][rows, :]), v_refs[h][rows, :].astype(BF16))
                     for h in range(hp))

    state = lax.fori_loop(0, qi, body, tuple(state))
    for h in range(hp):
        _, l, acc = state[h]
        o = acc / l
        o = o[:tq] - lam * o[tq:]
        o_ref[:, h * LANES:(h + 1) * LANES] = (_head_norm(o, g) * (1.0 - lam_init)).astype(o_ref.dtype)


def _head_specs(shape, hp, index):
    return [pl.BlockSpec(shape, (lambda n: lambda b, g, i: index(b, g * hp + n, i))(n)) for n in range(hp)]


def _diff_prompt(dqk, z, lam, gain, layer, lay):
    B, SEQ, H, NM, RP = lay["B"], lay["SEQ"], lay["DIFF_HEADS"], lay["N_META"], lay["RP"]
    tq = _divisor_tile(SEQ, 256, 16)
    nq = SEQ // tq
    hp = 2 if H % 2 == 0 else 1
    vcol = lay["DV_OFF"] // LANES
    nl = gain.shape[0]
    mrow = RP // NM
    vmem = 8 * hp * SEQ * LANES * 4 + 16 * 1024 * 1024
    in_specs = ([pl.BlockSpec((None,) + lam.shape[1:], lambda b, g, i: (layer, 0, 0)),
                 pl.BlockSpec((nl, LANES), lambda b, g, i: (0, 0))]
                + _head_specs((tq, LANES), hp, lambda b, h, i: (b * nq + i, h))
                + _head_specs((SEQ, LANES), hp, lambda b, h, i: (b, H + h))
                + _head_specs((SEQ, LANES), hp, lambda b, h, i: (b, vcol + h))
                + _head_specs((NM, LANES), hp, lambda b, h, i: (mrow + b, H + h))
                + _head_specs((NM, LANES), hp, lambda b, h, i: (mrow + b, vcol + h)))
    return pl.pallas_call(
        functools.partial(_diff_prompt_kernel, layer=layer, hp=hp),
        grid=(B, H // hp, nq),
        in_specs=in_specs,
        out_specs=pl.BlockSpec((tq, hp * LANES), lambda b, g, i: (b * nq + i, g)),
        out_shape=jax.ShapeDtypeStruct((RP, H * LANES), BF16),
        compiler_params=_params(("parallel", "parallel", "parallel"), vmem),
        name="diff_prompt",
    )(lam, gain, *([dqk] * hp + [dqk] * hp + [z] * hp + [dqk] * hp + [z] * hp))


def _diff_small_kernel(*refs, layer, heads, valid_rows, n_steps, pps):
    if n_steps:
        lam_ref, g_ref, qk_ref, v_ref = refs[1:5]
        kc_refs, vc_refs = refs[5:5 + pps], refs[5 + pps:5 + 2 * pps]
        o_ref, m_ref, l_ref, acc_ref = refs[5 + 2 * pps:]
    else:
        lam_ref, g_ref, qk_ref, v_ref, o_ref, m_ref, l_ref, acc_ref = refs
    T = qk_ref.shape[0]
    d = qk_ref.shape[1] // (4 * heads)
    scale = d ** -0.5
    step = pl.program_id(1)

    def q_map(m):
        return qk_ref[:, m * d:(m + 1) * d].astype(BF16)

    @pl.when(step == 0)
    def _():
        qidx = _iota((T, 1), 0)
        kidx = _iota((1, T), 1)
        visible = (kidx <= qidx) & (kidx < valid_rows)
        for m in range(2 * heads):
            h = m // 2
            km = qk_ref[:, (2 * heads + m) * d:(2 * heads + m + 1) * d].astype(BF16)
            s = jnp.where(visible, _nt_dot(q_map(m), km) * scale, NEG_BIG)
            mx = jnp.max(s, axis=-1, keepdims=True)
            p = jnp.exp(s - mx)
            rows = slice(m * T, (m + 1) * T)
            m_ref[rows, :] = mx
            l_ref[rows, :] = jnp.sum(p, axis=-1, keepdims=True)
            acc_ref[rows, :] = _dot(p.astype(BF16), v_ref[:, h * LANES:(h + 1) * LANES].astype(BF16))

    if n_steps:
        @pl.when(step > 0)
        def _():
            page = kc_refs[0].shape[-1]

            def keys_t(m):
                return jnp.concatenate([r[0, 0, m] for r in kc_refs], axis=1).astype(BF16)

            def values(h):
                return jnp.concatenate([r[0, 0, pl.ds(h, page, stride=heads), :] for r in vc_refs],
                                       axis=0).astype(BF16)

            s = jnp.concatenate([_dot(q_map(m), keys_t(m)) for m in range(2 * heads)],
                                axis=0) * scale
            mx = m_ref[...]
            m_new = jnp.maximum(mx, jnp.max(s, axis=-1, keepdims=True))
            alpha = jnp.exp(mx - m_new)
            p = jnp.exp(s - m_new)
            m_ref[...] = m_new
            l_ref[...] = alpha * l_ref[...] + jnp.sum(p, axis=-1, keepdims=True)
            pb = p.astype(BF16)
            for h in range(heads):
                rows = slice(2 * h * T, (2 * h + 2) * T)
                acc_ref[rows, :] = alpha[rows] * acc_ref[rows, :] + _dot(pb[rows], values(h))

    @pl.when(step == n_steps)
    def _():
        lam, lam_init = _diff_lambda(lam_ref, layer)
        o = acc_ref[...] / l_ref[...]
        g = g_ref[layer:layer + 1, :]
        for h in range(heads):
            oh = o[2 * h * T:(2 * h + 1) * T] - lam * o[(2 * h + 1) * T:(2 * h + 2) * T]
            o_ref[:, h * LANES:(h + 1) * LANES] = _head_norm(oh, g) * (1.0 - lam_init)


def _diff_small(dqk, z, lam, gain, layer, lay, *, row_block, rows, n_seq, valid_rows, cache=None):
    H = lay["DIFF_HEADS"]
    n_pages = lay["N_PAGES"] if cache is not None else 0
    pps = _pages_per_step(n_pages)
    n_steps = n_pages // pps
    nl = gain.shape[0]
    vcol = lay["DV_OFF"] // (H * LANES)

    def ix(f):
        return (lambda b, p, pt: f(b, p, pt)) if n_pages else (lambda b, p: f(b, p, None))

    in_specs = [pl.BlockSpec((None,) + lam.shape[1:], ix(lambda b, p, pt: (layer, 0, 0))),
                pl.BlockSpec((nl, LANES), ix(lambda b, p, pt: (0, 0))),
                pl.BlockSpec((rows, 2 * H * LANES), ix(lambda b, p, pt: (row_block + b, 0))),
                pl.BlockSpec((rows, H * LANES), ix(lambda b, p, pt: (row_block + b, vcol)))]
    args = [lam, gain, dqk, z]
    if n_pages:
        kc, vc, page_table = cache

        def page(i):
            return lambda b, p, pt: (layer, pt[b, (jnp.maximum(p, 1) - 1) * pps + i]) + (0,) * (kc.ndim - 2)

        def vpage(i):
            return lambda b, p, pt: (layer, pt[b, (jnp.maximum(p, 1) - 1) * pps + i], 0, 0)

        in_specs += [pl.BlockSpec((1, 1) + kc.shape[2:], page(i)) for i in range(pps)]
        in_specs += [pl.BlockSpec((1, 1) + vc.shape[2:], vpage(i)) for i in range(pps)]
        args += [kc] * pps + [vc] * pps
    n_rows = 2 * H * rows
    kernel = functools.partial(_diff_small_kernel, layer=layer, heads=H, valid_rows=valid_rows,
                               n_steps=n_steps, pps=pps)
    out_spec = pl.BlockSpec((rows, H * LANES), ix(lambda b, p, pt: (b, 0)))
    scratch = [pltpu.VMEM((n_rows, 1), F32), pltpu.VMEM((n_rows, 1), F32), pltpu.VMEM((n_rows, LANES), F32)]
    out_shape = jax.ShapeDtypeStruct((n_seq * rows, H * LANES), F32)
    params = _params(("parallel", "arbitrary"), 32 * 1024 * 1024)
    if n_pages:
        return pl.pallas_call(
            kernel,
            grid_spec=pltpu.PrefetchScalarGridSpec(
                num_scalar_prefetch=1, grid=(n_seq, n_steps + 1), in_specs=in_specs,
                out_specs=out_spec, scratch_shapes=scratch),
            out_shape=out_shape, compiler_params=params, name="diff_sample",
        )(page_table, *args)
    return pl.pallas_call(
        kernel, grid=(n_seq, 1), in_specs=in_specs, out_specs=out_spec, scratch_shapes=scratch,
        out_shape=out_shape, compiler_params=params, name="diff_meta",
    )(*args)


def _log_sigmoid_pair(z):
    log_beta = jnp.minimum(z, 0.0) - jnp.log(1.0 + jnp.exp(-jnp.abs(z)))
    return log_beta, log_beta - z


def _sb_block(q, k, v, scale, carry, acc, visible, triu):
    z = _nt_dot(q, k) * scale
    log_beta, log_1mb = _log_sigmoid_pair(z)
    if visible is not None:
        log_1mb = jnp.where(visible, log_1mb, 0.0)
    tail = None
    for part in _split_bf16(log_1mb, 2):
        t = _dot(part, triu)
        tail = t if tail is None else tail + t
    w = log_beta + tail + carry
    if visible is not None:
        w = jnp.where(visible, w, NEG_BIG)
    acc = acc + _dot(jnp.exp(w).astype(BF16), v)
    return carry + jnp.sum(log_1mb, axis=-1, keepdims=True), acc


def _triu(n):
    return (_iota((n, n), 0) > _iota((n, n), 1)).astype(BF16)


def _sb_prompt_kernel(g_ref, *refs, layer, hp):
    q_refs, k_refs, v_refs, km_refs, vm_refs = (refs[i * hp:(i + 1) * hp] for i in range(5))
    o_ref = refs[5 * hp]
    qi = pl.program_id(2)
    tq, d = q_refs[0].shape
    scale = d ** -0.5
    qs = [r[...].astype(BF16) for r in q_refs]
    triu = _triu(tq)

    def chunk(h, rows, carry, acc, visible):
        return _sb_block(qs[h], k_refs[h][rows, :].astype(BF16), v_refs[h][rows, :].astype(BF16), scale,
                         carry, acc, visible, triu)

    diag = pl.ds(pl.multiple_of(qi * tq, tq), tq)
    strictly_before = _iota((1, tq), 1) < _iota((tq, 1), 0)
    state = tuple(chunk(h, diag, jnp.zeros((tq, 1), F32), jnp.zeros((tq, d), F32), strictly_before)
                  for h in range(hp))

    def body(i, state):
        rows = pl.ds(pl.multiple_of((qi - 1 - i) * tq, tq), tq)
        return tuple(chunk(h, rows, state[h][0], state[h][1], None) for h in range(hp))

    state = lax.fori_loop(0, qi, body, state)
    g = g_ref[layer:layer + 1, :]
    meta_triu = _triu(km_refs[0].shape[0])
    for h in range(hp):
        _, acc = _sb_block(qs[h], km_refs[h][...].astype(BF16), vm_refs[h][...].astype(BF16), scale,
                           state[h][0], state[h][1], None, meta_triu)
        o_ref[:, h * LANES:(h + 1) * LANES] = _head_norm(acc, g).astype(o_ref.dtype)


def _sb_prompt(z, gain, layer, lay):
    B, SEQ, H, NM, RP = lay["B"], lay["SEQ"], lay["SB_HEADS"], lay["N_META"], lay["RP"]
    tq = _divisor_tile(SEQ, 256, 16)
    nq = SEQ // tq
    hp = 2 if H % 2 == 0 else 1
    qc, kc, vc = (lay["SQ_OFF"] // LANES, lay["SK_OFF"] // LANES, lay["SV_OFF"] // LANES)
    nl = gain.shape[0]
    mrow = RP // NM
    vmem = 8 * hp * SEQ * LANES * 4 + 24 * 1024 * 1024
    in_specs = ([pl.BlockSpec((nl, LANES), lambda b, g, i: (0, 0))]
                + _head_specs((tq, LANES), hp, lambda b, h, i: (b * nq + i, qc + h))
                + _head_specs((SEQ, LANES), hp, lambda b, h, i: (b, kc + h))
                + _head_specs((SEQ, LANES), hp, lambda b, h, i: (b, vc + h))
                + _head_specs((NM, LANES), hp, lambda b, h, i: (mrow + b, kc + h))
                + _head_specs((NM, LANES), hp, lambda b, h, i: (mrow + b, vc + h)))
    return pl.pallas_call(
        functools.partial(_sb_prompt_kernel, layer=layer, hp=hp),
        grid=(B, H // hp, nq),
        in_specs=in_specs,
        out_specs=pl.BlockSpec((tq, hp * LANES), lambda b, g, i: (b * nq + i, g)),
        out_shape=jax.ShapeDtypeStruct((RP, H * LANES), BF16),
        compiler_params=_params(("parallel", "parallel", "parallel"), vmem),
        name="sb_prompt",
    )(gain, *([z] * (5 * hp)))


def _sb_small_kernel(*refs, layer, heads, pieces, valid_rows, n_steps, pps):
    if n_steps:
        refs = refs[1:]
    g_ref = refs[0]
    q_ref, k_ref, v_ref = (refs[1 + i * pieces:1 + (i + 1) * pieces] for i in range(3))
    rest = refs[1 + 3 * pieces:]
    if n_steps:
        kc_refs, vc_refs = rest[:pps], rest[pps:2 * pps]
        o_ref, carry_ref, acc_ref = rest[2 * pps:]
    else:
        o_ref, carry_ref, acc_ref = rest
    T = o_ref.shape[0]
    d = LANES
    scale = d ** -0.5
    step = pl.program_id(1)
    per_piece = heads // pieces

    def head(piece_refs, h):
        c = (h % per_piece) * d
        return piece_refs[h // per_piece][:, c:c + d].astype(BF16)

    @pl.when(step == 0)
    def _():
        qidx = _iota((T, 1), 0)
        kidx = _iota((1, T), 1)
        visible = (kidx < qidx) & (kidx < valid_rows)
        triu = _triu(T)
        for h in range(heads):
            carry, acc = _sb_block(head(q_ref, h), head(k_ref, h), head(v_ref, h), scale,
                                   jnp.zeros((T, 1), F32), jnp.zeros((T, d), F32), visible, triu)
            carry_ref[h * T:(h + 1) * T, :] = carry
            acc_ref[h * T:(h + 1) * T, :] = acc

    if n_steps:
        @pl.when(step > 0)
        def _():
            page = kc_refs[0].shape[2] // heads

            def rows_of(page_refs, h):
                return jnp.concatenate([r[0, 0, pl.ds(h, page, stride=heads), :] for r in page_refs],
                                       axis=0).astype(BF16)

            z = jnp.concatenate([_nt_dot(head(q_ref, h), rows_of(kc_refs, h)) for h in range(heads)],
                                axis=0) * scale
            log_beta, log_1mb = _log_sigmoid_pair(z)
            triu = _triu(pps * page)
            tail = None
            for part in _split_bf16(log_1mb, 2):
                t = _dot(part, triu)
                tail = t if tail is None else tail + t
            a = jnp.exp(log_beta + tail + carry_ref[...]).astype(BF16)
            carry_ref[...] += jnp.sum(log_1mb, axis=-1, keepdims=True)
            for h in range(heads):
                rows = slice(h * T, (h + 1) * T)
                acc_ref[rows, :] += _dot(a[rows], rows_of(vc_refs, h))

    @pl.when(step == n_steps)
    def _():
        o = _head_norm(acc_ref[...], g_ref[layer:layer + 1, :])
        for h in range(heads):
            o_ref[:, h * d:(h + 1) * d] = o[h * T:(h + 1) * T]


def _sb_small(z, gain, layer, lay, *, row_block, rows, n_seq, valid_rows, cache=None):
    H = lay["SB_HEADS"]
    n_pages = lay["N_PAGES"] if cache is not None else 0
    nl = gain.shape[0]
    width = H * LANES
    cw = math.gcd(lay["SQ_OFF"], width)
    pieces = width // cw

    def ix(f):
        return (lambda b, p, pt: f(b, p, pt)) if n_pages else (lambda b, p: f(b, p, None))

    in_specs = [pl.BlockSpec((nl, LANES), ix(lambda b, p, pt: (0, 0)))]
    in_specs += [pl.BlockSpec((rows, cw), ix((lambda c: lambda b, p, pt: (row_block + b, c))(off // cw + i)))
                 for off in (lay["SQ_OFF"], lay["SK_OFF"], lay["SV_OFF"]) for i in range(pieces)]
    args = [gain] + [z] * (3 * pieces)
    pps = _pages_per_step(n_pages)
    n_steps = n_pages // pps
    if n_pages:
        kc, vc, page_table = cache

        def page(i):
            return lambda b, p, pt: (layer, pt[b, n_pages - jnp.maximum(p, 1) * pps + i], 0, 0)

        in_specs += [pl.BlockSpec((1, 1) + kc.shape[2:], page(i)) for i in range(pps)]
        in_specs += [pl.BlockSpec((1, 1) + vc.shape[2:], page(i)) for i in range(pps)]
        args += [kc] * pps + [vc] * pps
    kernel = functools.partial(_sb_small_kernel, layer=layer, heads=H, pieces=pieces,
                               valid_rows=valid_rows, n_steps=n_steps, pps=pps)
    out_spec = pl.BlockSpec((rows, width), ix(lambda b, p, pt: (b, 0)))
    scratch = [pltpu.VMEM((H * rows, 1), F32), pltpu.VMEM((H * rows, LANES), F32)]
    out_shape = jax.ShapeDtypeStruct((n_seq * rows, width), F32)
    params = _params(("parallel", "arbitrary"), 32 * 1024 * 1024)
    if n_pages:
        return pl.pallas_call(
            kernel,
            grid_spec=pltpu.PrefetchScalarGridSpec(
                num_scalar_prefetch=1, grid=(n_seq, n_steps + 1), in_specs=in_specs,
                out_specs=out_spec, scratch_shapes=scratch),
            out_shape=out_shape, compiler_params=params, name="sb_sample",
        )(page_table, *args)
    return pl.pallas_call(
        kernel, grid=(n_seq, 1), in_specs=in_specs, out_specs=out_spec, scratch_shapes=scratch,
        out_shape=out_shape, compiler_params=params, name="sb_meta",
    )(*args)


def _rope_tables(pos, d_map, rope_dim):
    half = rope_dim // 2
    inv = ROPE_THETA ** (-jnp.arange(half, dtype=F32) / half)
    ang = pos.astype(F32)[:, None] * inv[None, :]
    cos, sin = jnp.cos(ang), jnp.sin(ang)
    rows = pos.shape[0]
    pad = jnp.zeros((rows, d_map - rope_dim), F32)
    zero = jnp.zeros((rows, half), F32)
    c = jnp.concatenate([cos, cos, pad + 1.0], axis=1)
    s1 = jnp.concatenate([-sin, zero, pad], axis=1)
    s2 = jnp.concatenate([zero, sin, pad], axis=1)
    rep = LANES // d_map
    return tuple(jnp.tile(t, (1, rep)) for t in (c, s1, s2))


def kernel(x_prompt, x_sample, cache_diff_k, cache_diff_v, cache_sb_k, cache_sb_v, state_hgrn,
           page_table, meta_tokens, norm_mix, norm_ffn, norm_final, w_in, hg_lower_bounds, hg_norm,
           diff_lambda, diff_subln, sb_norm, w_out, w_gate_up, w_down):
    B, SEQ, D = x_prompt.shape
    DB, DEC_SEQ, _ = x_sample.shape
    DEPTH = w_in.shape[0]
    NM = meta_tokens.shape[0]
    HGH = state_hgrn.shape[2]
    DH = cache_diff_v.shape[3]
    DQK = cache_diff_k.shape[4]
    SH = cache_sb_k.shape[3]
    PAGE = cache_sb_k.shape[2]
    NP = page_table.shape[1]
    assert DEC_SEQ <= SAMPLE_ROWS and NM % 16 == 0 and SEQ % NM == 0 and 2 * DQK == LANES
    assert state_hgrn.shape[3:] == (LANES, LANES) and HGH == DH
    HGW, DW, SW = HGH * LANES, DH * LANES, SH * LANES
    RP, RM, RS = B * SEQ, B * NM, DB * SAMPLE_ROWS
    lay = dict(B=B, SEQ=SEQ, DB=DB, DEC_SEQ=DEC_SEQ, N_META=NM, HG_HEADS=HGH, DIFF_HEADS=DH, SB_HEADS=SH,
               N_PAGES=NP, RP=RP, RM=RM, RS=RS,
               DV_OFF=4 * HGW + 2 * DW, SQ_OFF=4 * HGW + 3 * DW, SK_OFF=4 * HGW + 3 * DW + SW,
               SV_OFF=4 * HGW + 3 * DW + 2 * SW)

    sample = jnp.pad(x_sample, ((0, 0), (0, SAMPLE_ROWS - DEC_SEQ), (0, 0)))
    x = jnp.concatenate([x_prompt.reshape(RP, D),
                         jnp.broadcast_to(meta_tokens[None], (B, NM, D)).reshape(RM, D),
                         sample.reshape(RS, D)], axis=0)
    pos = jnp.concatenate([jnp.tile(NM + jnp.arange(SEQ, dtype=jnp.int32), B),
                           jnp.tile(jnp.arange(NM, dtype=jnp.int32), B),
                           jnp.tile(NP * PAGE + jnp.arange(SAMPLE_ROWS, dtype=jnp.int32), DB)])
    tables = _rope_tables(pos, DQK, DQK // 4)

    ck_d = jnp.transpose(cache_diff_k, (0, 1, 3, 4, 2))
    cv_d = cache_diff_v.reshape(cache_diff_v.shape[:2] + (PAGE * DH, LANES))
    ck_s = cache_sb_k.reshape(cache_sb_k.shape[:2] + (PAGE * SH, LANES))
    cv_s = cache_sb_v.reshape(cache_sb_v.shape[:2] + (PAGE * SH, LANES))

    w_in_b, w_out_b, w_gu_b, w_dn_b = w_in, w_out, w_gate_up, w_down
    g_mix = norm_mix.reshape(DEPTH, 1, D)
    g_ffn = norm_ffn.reshape(DEPTH, 1, D)
    g_fin = norm_final.reshape(1, 1, D)
    meta_blk = RP // NM
    samp_blk = (RP + RM) // SAMPLE_ROWS
    small_meta = dict(row_block=meta_blk, rows=NM, n_seq=B, valid_rows=NM)
    small_samp = dict(row_block=samp_blk, rows=SAMPLE_ROWS, n_seq=DB, valid_rows=DEC_SEQ)

    outs = {k: [] for k in ("pk_d", "pv_d", "pk_s", "pv_s", "ps_h", "sk_d", "sv_d", "sk_s", "sv_s", "ss_h")}

    def cache_rows(a, off, width, heads):
        a = a[:, off:off + width]
        p = jnp.concatenate([a[RP:RP + RM].reshape(B, NM, heads, -1), a[:RP].reshape(B, SEQ, heads, -1)], axis=1)
        s = a[RP + RM:].reshape(DB, SAMPLE_ROWS, heads, -1)[:, :DEC_SEQ]
        return p, s

    for l in range(DEPTH):
        hn = _rmsnorm(x, g_mix, l, BF16)
        z = _matmul(hn, w_in_b, l)
        dqk = _rope(z, 2, 2 * DW, tables)

        hg_m, hg_p, s_p = _hgrn_prompt(z, hg_lower_bounds, hg_norm, l, lay)
        hg_s, s_s = _hgrn_sample(z, state_hgrn, hg_lower_bounds, hg_norm, l, lay)
        df_p = _diff_prompt(dqk, z, diff_lambda, diff_subln, l, lay)
        df_m = _diff_small(dqk, z, diff_lambda, diff_subln, l, lay, **small_meta)
        df_s = _diff_small(dqk, z, diff_lambda, diff_subln, l, lay, **small_samp,
                           cache=(ck_d, cv_d, page_table))
        sb_p = _sb_prompt(z, sb_norm, l, lay)
        sb_m = _sb_small(z, sb_norm, l, lay, **small_meta)
        sb_s = _sb_small(z, sb_norm, l, lay, **small_samp, cache=(ck_s, cv_s, page_table))

        mix = jnp.concatenate([
            jnp.concatenate([hg_p, df_p, sb_p], axis=1),
            jnp.concatenate([hg_m, df_m.astype(BF16), sb_m.astype(BF16)], axis=1),
            jnp.concatenate([hg_s.astype(BF16), df_s.astype(BF16), sb_s.astype(BF16)], axis=1)], axis=0)
        x = _matmul(mix, w_out_b, l, res=x)
        h = _ffn_up(_rmsnorm(x, g_ffn, l, BF16), w_gu_b, l)
        x = _matmul(h, w_dn_b, l, res=x)

        for name_p, name_s, (p, s) in (
                ("pk_d", "sk_d", cache_rows(dqk, DW, DW, 2 * DH)),
                ("pv_d", "sv_d", cache_rows(z, lay["DV_OFF"], DW, DH)),
                ("pk_s", "sk_s", cache_rows(z, lay["SK_OFF"], SW, SH)),
                ("pv_s", "sv_s", cache_rows(z, lay["SV_OFF"], SW, SH))):
            outs[name_p].append(p)
            outs[name_s].append(s)
        outs["ps_h"].append(s_p)
        outs["ss_h"].append(s_s)

    y = _rmsnorm(x, g_fin, 0, F32)
    y_prompt = y[:RP].reshape(B, SEQ, D)
    y_sample = y[RP + RM:].reshape(DB, SAMPLE_ROWS, D)[:, :DEC_SEQ]
    st = lambda k: jnp.stack(outs[k], axis=0)
    return (y_prompt, y_sample, st("pk_d"), st("pv_d"), st("pk_s"), st("pv_s"), st("ps_h"),
            st("sk_d"), st("sv_d"), st("sk_s"), st("sv_s"), st("ss_h"))
```

```python
import functools
import math

import jax
import jax.numpy as jnp
from jax import lax
from jax.experimental import pallas as pl
from jax.experimental.pallas import tpu as pltpu

F32 = jnp.float32
BF16 = jnp.bfloat16
EPS = 1e-6
NEG_BIG = -1e30
ROPE_THETA = 500000.0
LANES = 128
SAMPLE_ROWS = 8
V7X_VMEM_BYTES = 64 * 1024 * 1024
VMEM_CAP = V7X_VMEM_BYTES - 8 * 1024 * 1024


def _params(semantics, vmem_bytes):
    limit = int(min(max(vmem_bytes, 16 * 1024 * 1024), VMEM_CAP))
    return pltpu.CompilerParams(dimension_semantics=semantics, vmem_limit_bytes=limit)


def _divisor_tile(n, target, multiple):
    best = None
    for t in range(multiple, min(n, target) + 1, multiple):
        if n % t == 0:
            best = t
    return best if best is not None else n


def _pages_per_step(n_pages):
    return 2 if n_pages > 0 and n_pages % 2 == 0 else 1


def _nt_dot(a, b):
    return lax.dot_general(a, b, (((1,), (1,)), ((), ())), preferred_element_type=F32)


def _dot(a, b):
    return jnp.dot(a, b, preferred_element_type=F32)


def _iota(shape, dim):
    return lax.broadcasted_iota(jnp.int32, shape, dim)


def _split_bf16(x, parts):
    out = []
    r = x
    for i in range(parts):
        p = r.astype(BF16)
        out.append(p)
        if i + 1 < parts:
            r = r - p.astype(F32)
    return out


def _sigmoid_pair(z):
    e = jnp.exp(-jnp.abs(z))
    r = 1.0 / (1.0 + e)
    er = e * r
    pos = z >= 0
    return jnp.where(pos, r, er), jnp.where(pos, er, r)


def _silu(z):
    return z * _sigmoid_pair(z)[0]


def _head_norm(o, g):
    return o * lax.rsqrt(jnp.mean(o * o, axis=-1, keepdims=True) + EPS) * g


def _rmsnorm_kernel(x_ref, g_ref, o_ref):
    x = x_ref[...]
    y = x * lax.rsqrt(jnp.mean(x * x, axis=-1, keepdims=True) + EPS)
    o_ref[...] = (y * g_ref[...]).astype(o_ref.dtype)


def _rmsnorm(x, gains, layer, out_dtype):
    M, D = x.shape
    tr = _divisor_tile(M, 256, 16)
    vmem = 2 * tr * D * (4 + jnp.dtype(out_dtype).itemsize) + 3 * tr * D * 4
    return pl.pallas_call(
        _rmsnorm_kernel,
        grid=(M // tr,),
        in_specs=[pl.BlockSpec((tr, D), lambda i: (i, 0)),
                  pl.BlockSpec((None, 1, D), lambda i: (layer, 0, 0))],
        out_specs=pl.BlockSpec((tr, D), lambda i: (i, 0)),
        out_shape=jax.ShapeDtypeStruct((M, D), out_dtype),
        compiler_params=_params(("parallel",), vmem),
        name="rmsnorm",
    )(x, gains)


def _rope_kernel(x_ref, c_ref, s1_ref, s2_ref, o_ref):
    c = c_ref[...]
    s1 = s1_ref[...]
    s2 = s2_ref[...]
    for j in range(x_ref.shape[1] // LANES):
        x = x_ref[:, j * LANES:(j + 1) * LANES]
        up = pltpu.roll(x, LANES - 8, 1)
        dn = pltpu.roll(x, 8, 1)
        o_ref[:, j * LANES:(j + 1) * LANES] = x * c + up * s1 + dn * s2


def _rope(z, col_block, width, tables):
    M = z.shape[0]
    tr = _divisor_tile(M, 512, 8)
    tab = pl.BlockSpec((tr, LANES), lambda i: (i, 0))
    vmem = 4 * tr * width * 4 + 6 * tr * LANES * 4 + 4 * tr * LANES * 4
    return pl.pallas_call(
        _rope_kernel,
        grid=(M // tr,),
        in_specs=[pl.BlockSpec((tr, width), lambda i: (i, col_block)), tab, tab, tab],
        out_specs=pl.BlockSpec((tr, width), lambda i: (i, 0)),
        out_shape=jax.ShapeDtypeStruct((M, width), F32),
        compiler_params=_params(("parallel",), vmem),
        name="rope",
    )(z, *tables)


def _matmul_kernel(a_ref, w_ref, *rest, nk, has_res):
    o_ref = rest[-1]
    p = _dot(a_ref[...], w_ref[...].astype(BF16))
    if has_res:
        first = p + rest[0][...]
    else:
        first = p
    if nk == 1:
        o_ref[...] = first
    else:
        k = pl.program_id(2)

        @pl.when(k == 0)
        def _():
            o_ref[...] = first

        @pl.when(k > 0)
        def _():
            o_ref[...] += p


def _matmul_tiles(M, K, N):
    wide = K > 4096
    bm = _divisor_tile(M, 832 if wide else 1664, 16)
    bn = _divisor_tile(N, 256 if wide else 512, LANES)
    return bm, K, bn


def _matmul(a, w, layer, res=None):
    M, K = a.shape
    N = w.shape[2]
    bm, bk, bn = _matmul_tiles(M, K, N)
    nk = K // bk
    a_bufs = 1 if nk == 1 else 2
    a_kw = dict(pipeline_mode=pl.Buffered(1)) if nk == 1 else {}
    in_specs = [pl.BlockSpec((bm, bk), lambda i, j, k: (i, k), **a_kw),
                pl.BlockSpec((None, bk, bn), lambda i, j, k: (layer, k, j))]
    args = [a, w]
    if res is not None:
        in_specs.append(pl.BlockSpec((bm, bn), lambda i, j, k: (i, j)))
        args.append(res)
    vmem = (a_bufs * bm * bk * 2 + 2 * bk * bn * w.dtype.itemsize + bk * bn * 2
            + (4 if res is not None else 2) * bm * bn * 4 + 2 * bm * bn * 4)
    return pl.pallas_call(
        functools.partial(_matmul_kernel, nk=nk, has_res=res is not None),
        grid=(M // bm, N // bn, nk),
        in_specs=in_specs,
        out_specs=pl.BlockSpec((bm, bn), lambda i, j, k: (i, j)),
        out_shape=jax.ShapeDtypeStruct((M, N), F32),
        compiler_params=_params(("parallel", "parallel", "arbitrary"), vmem),
        name="matmul",
    )(*args)


def _ffn_up_kernel(a_ref, wg_ref, wu_ref, o_ref):
    a = a_ref[...]
    g = _dot(a, wg_ref[...].astype(BF16))
    u = _dot(a, wu_ref[...].astype(BF16))
    o_ref[...] = (_silu(g) * u).astype(o_ref.dtype)


def _ffn_up(a, w_gu, layer):
    M, K = a.shape
    F = w_gu.shape[2] // 2
    bm = _divisor_tile(M, 1664, 16)
    bn = _divisor_tile(F, 256, LANES)
    nf = F // bn
    vmem = (bm * K * 2 + 4 * K * bn * w_gu.dtype.itemsize + 2 * K * bn * 2 + 2 * bm * bn * 2
            + 4 * bm * bn * 4)
    return pl.pallas_call(
        _ffn_up_kernel,
        grid=(M // bm, nf),
        in_specs=[pl.BlockSpec((bm, K), lambda i, j: (i, 0), pipeline_mode=pl.Buffered(1)),
                  pl.BlockSpec((None, K, bn), lambda i, j: (layer, 0, j)),
                  pl.BlockSpec((None, K, bn), lambda i, j: (layer, 0, nf + j))],
        out_specs=pl.BlockSpec((bm, bn), lambda i, j: (i, j)),
        out_shape=jax.ShapeDtypeStruct((M, F), BF16),
        compiler_params=_params(("parallel", "parallel"), vmem),
        name="ffn_up",
    )(a, w_gu, w_gu)


def _lower_bound(lbraw_ref, layer):
    raw = lbraw_ref[...]
    e = jnp.exp(raw - jnp.max(raw, axis=0, keepdims=True))
    total = jnp.sum(e, axis=0, keepdims=True)
    if layer == 0:
        return jnp.zeros_like(total)
    return jnp.sum(e[1:layer + 1], axis=0, keepdims=True) / total


def _gla_chunk(st, hq, hf, hi, lb, *, valid_rows, sub):
    C, K = hq.shape
    sig, sig_neg = _sigmoid_pair(hf)
    q = _silu(hq) * (K ** -0.5)
    gl = jnp.log(lb + (1.0 - lb) * sig)
    k = (1.0 - lb) * sig_neg
    v = hi
    if valid_rows < C:
        ok = _iota((C, 1), 0) < valid_rows
        gl = jnp.where(ok, gl, 0.0)
        k = jnp.where(ok, k, 0.0)

    row = _iota((C, C), 0)
    col = _iota((C, C), 1)
    tril = (col <= row).astype(BF16)
    G = None
    for part in _split_bf16(gl, 3):
        t = _dot(tril, part)
        G = t if G is None else G + t

    o = _nt_dot((q * jnp.exp(G)).astype(BF16), st.astype(BF16))

    A = jnp.zeros((C, C), F32)
    b = sub
    while b < C:
        nb = C // b
        ends = [G[r * b + b - 1:r * b + b] for r in range(nb)]
        e_cur = jnp.concatenate([jnp.broadcast_to(e, (b, K)) for e in ends], axis=0)
        e_prev = jnp.concatenate([jnp.zeros((b, K), F32)]
                                 + [jnp.broadcast_to(e, (b, K)) for e in ends[:-1]], axis=0)
        qb = (q * jnp.exp(G - e_prev)).astype(BF16)
        kb = (k * jnp.exp(e_cur - G)).astype(BF16)
        sh = int(math.log2(b))
        in_level = (((row >> (sh + 1)) == (col >> (sh + 1)))
                    & (((row >> sh) & 1) == 1) & (((col >> sh) & 1) == 0))
        A = jnp.where(in_level, _nt_dot(qb, kb), A)
        b *= 2

    lane = _iota((sub, C), 1)
    trow = _iota((sub, 1), 0)
    blocks = []
    for i in range(C // sub):
        Gi = G[i * sub:(i + 1) * sub]
        qi = q[i * sub:(i + 1) * sub]
        Ai = A[i * sub:(i + 1) * sub]
        for s in range(sub):
            r = i * sub + s
            expo = jnp.where(trow >= s, Gi - G[r:r + 1], NEG_BIG)
            colv = jnp.sum(qi * k[r:r + 1] * jnp.exp(expo), axis=-1, keepdims=True)
            Ai = jnp.where(lane == r, colv, Ai)
        blocks.append(Ai)
    A = jnp.concatenate(blocks, axis=0) if len(blocks) > 1 else blocks[0]
    o = o + _dot(A.astype(BF16), v.astype(BF16))

    g_last = G[C - 1:C]
    k_dec = (k * jnp.exp(g_last - G)).astype(BF16)
    st_new = st * jnp.exp(g_last) + lax.dot_general(
        v.astype(BF16), k_dec, (((0,), (0,)), ((), ())), preferred_element_type=F32)
    return st_new, o


def _hgrn_prompt_kernel(lbraw_ref, gn_ref, mq_ref, mf_ref, mi_ref, mg_ref,
                        hq_ref, hf_ref, hi_ref, hg_ref, om_ref, op_ref, s_ref, *, layer, chunk):
    lb = _lower_bound(lbraw_ref, layer)
    gn = gn_ref[layer:layer + 1, :]
    n_meta = mq_ref.shape[0]
    K = hq_ref.shape[1]

    st, o = _gla_chunk(jnp.zeros((K, K), F32), mq_ref[...], mf_ref[...], mi_ref[...], lb,
                       valid_rows=n_meta, sub=min(16, n_meta))
    om_ref[...] = (_head_norm(o, gn) * _silu(mg_ref[...])).astype(om_ref.dtype)

    def body(ci, st):
        rows = pl.ds(pl.multiple_of(ci * chunk, chunk), chunk)
        st, o = _gla_chunk(st, hq_ref[rows, :], hf_ref[rows, :], hi_ref[rows, :], lb,
                           valid_rows=chunk, sub=16)
        op_ref[rows, :] = (_head_norm(o, gn) * _silu(hg_ref[rows, :])).astype(op_ref.dtype)
        return st

    st = lax.fori_loop(0, hq_ref.shape[0] // chunk, body, st)
    s_ref[...] = st.T


def _hgrn_prompt(z, lb_raw, gnorm, layer, lay):
    B, SEQ, H, NM, RP, RM = lay["B"], lay["SEQ"], lay["HG_HEADS"], lay["N_META"], lay["RP"], lay["RM"]
    chunk = _divisor_tile(SEQ, 128, 16)
    nl = lb_raw.shape[0]

    def col(k):
        return lambda b, h: (b, k * H + h)

    def mcol(k):
        return lambda b, h: (RP // NM + b, k * H + h)

    prompt = [pl.BlockSpec((SEQ, LANES), col(k)) for k in range(4)]
    meta = [pl.BlockSpec((NM, LANES), mcol(k)) for k in range(4)]
    vmem = 16 * SEQ * LANES * 4 + 8 * 1024 * 1024
    return pl.pallas_call(
        functools.partial(_hgrn_prompt_kernel, layer=layer, chunk=chunk),
        grid=(B, H),
        in_specs=[pl.BlockSpec((nl, LANES), lambda b, h: (0, h)),
                  pl.BlockSpec((nl, LANES), lambda b, h: (0, 0))] + meta + prompt,
        out_specs=[pl.BlockSpec((NM, LANES), lambda b, h: (b, h)),
                   pl.BlockSpec((SEQ, LANES), lambda b, h: (b, h)),
                   pl.BlockSpec((None, None, LANES, LANES), lambda b, h: (b, h, 0, 0))],
        out_shape=[jax.ShapeDtypeStruct((RM, H * LANES), BF16),
                   jax.ShapeDtypeStruct((RP, H * LANES), BF16),
                   jax.ShapeDtypeStruct((B, H, LANES, LANES), F32)],
        compiler_params=_params(("parallel", "parallel"), vmem),
        name="hgrn_prompt",
    )(lb_raw, gnorm, z, z, z, z, z, z, z, z)


def _hgrn_sample_kernel(lbraw_ref, gn_ref, s0_ref, hq_ref, hf_ref, hi_ref, hg_ref, o_ref, s_ref,
                        *, layer, valid_rows):
    lb = _lower_bound(lbraw_ref, layer)
    gn = gn_ref[layer:layer + 1, :]
    st, o = _gla_chunk(s0_ref[...].T, hq_ref[...], hf_ref[...], hi_ref[...], lb,
                       valid_rows=valid_rows, sub=hq_ref.shape[0])
    o_ref[...] = _head_norm(o, gn) * _silu(hg_ref[...])
    s_ref[...] = st.T


def _hgrn_sample(z, state, lb_raw, gnorm, layer, lay):
    DB, H, RS = lay["DB"], lay["HG_HEADS"], lay["RS"]
    base = (lay["RP"] + lay["RM"]) // SAMPLE_ROWS
    nl = lb_raw.shape[0]
    rows = [pl.BlockSpec((SAMPLE_ROWS, LANES), (lambda k: lambda b, h: (base + b, k * H + h))(k))
            for k in range(4)]
    return pl.pallas_call(
        functools.partial(_hgrn_sample_kernel, layer=layer, valid_rows=lay["DEC_SEQ"]),
        grid=(DB, H),
        in_specs=[pl.BlockSpec((nl, LANES), lambda b, h: (0, h)),
                  pl.BlockSpec((nl, LANES), lambda b, h: (0, 0)),
                  pl.BlockSpec((None, None, None, LANES, LANES), lambda b, h: (layer, b, h, 0, 0))] + rows,
        out_specs=[pl.BlockSpec((SAMPLE_ROWS, LANES), lambda b, h: (b, h)),
                   pl.BlockSpec((None, None, LANES, LANES), lambda b, h: (b, h, 0, 0))],
        out_shape=[jax.ShapeDtypeStruct((RS, H * LANES), F32),
                   jax.ShapeDtypeStruct((DB, H, LANES, LANES), F32)],
        compiler_params=_params(("parallel", "parallel"), 16 * 1024 * 1024),
        name="hgrn_sample",
    )(lb_raw, gnorm, state, z, z, z, z)


def _diff_lambda(lam_ref, layer):
    lp = lam_ref[...]
    a = jnp.sum(lp[0:1] * lp[1:2], axis=-1, keepdims=True)
    b = jnp.sum(lp[2:3] * lp[3:4], axis=-1, keepdims=True)
    lam_init = 0.8 - 0.6 * math.exp(-0.3 * layer)
    return jnp.exp(a) - jnp.exp(b) + lam_init, lam_init


def _softmax_step(carry, s, v):
    mx, l, acc = carry
    m_new = jnp.maximum(mx, jnp.max(s, axis=-1, keepdims=True))
    alpha = jnp.exp(mx - m_new)
    p = jnp.exp(s - m_new)
    return (m_new, alpha * l + jnp.sum(p, axis=-1, keepdims=True),
            alpha * acc + _dot(p.astype(BF16), v))


def _diff_prompt_kernel(lam_ref, g_ref, *refs, layer, hp):
    q_refs, k_refs, v_refs, km_refs, vm_refs = (refs[i * hp:(i + 1) * hp] for i in range(5))
    o_ref = refs[5 * hp]
    qi = pl.program_id(2)
    tq, width = q_refs[0].shape
    d = width // 2
    scale = d ** -0.5
    lam, lam_init = _diff_lambda(lam_ref, layer)
    g = g_ref[layer:layer + 1, :]
    qs = [[r[:, m * d:(m + 1) * d].astype(BF16) for m in range(2)] for r in q_refs]

    def scores(h, k):
        return jnp.concatenate(
            [_nt_dot(qs[h][m], k[:, m * d:(m + 1) * d].astype(BF16)) for m in range(2)], axis=0) * scale

    state = []
    for h in range(hp):
        s = scores(h, km_refs[h][...])
        mx = jnp.max(s, axis=-1, keepdims=True)
        p = jnp.exp(s - mx)
        state.append((mx, jnp.sum(p, axis=-1, keepdims=True),
                      _dot(p.astype(BF16), vm_refs[h][...].astype(BF16))))

    diag = pl.ds(pl.multiple_of(qi * tq, tq), tq)
    causal = _iota((1, tq), 1) <= _iota((tq, 1), 0)
    causal = jnp.concatenate([causal, causal], axis=0)
    for h in range(hp):
        s = jnp.where(causal, scores(h, k_refs[h][diag, :]), NEG_BIG)
        state[h] = _softmax_step(state[h], s, v_refs[h][diag, :].astype(BF16))

    def body(j, state):
        rows = pl.ds(pl.multiple_of(j * tq, tq), tq)
        return tuple(_softmax_step(state[h], scores(h, k_refs[h][rows, :]), v_refs[h][rows, :].astype(BF16))
                     for h in range(hp))

    state = lax.fori_loop(0, qi, body, tuple(state))
    for h in range(hp):
        _, l, acc = state[h]
        o = acc / l
        o = o[:tq] - lam * o[tq:]
        o_ref[:, h * LANES:(h + 1) * LANES] = (_head_norm(o, g) * (1.0 - lam_init)).astype(o_ref.dtype)


def _head_specs(shape, hp, index):
    return [pl.BlockSpec(shape, (lambda n: lambda b, g, i: index(b, g * hp + n, i))(n)) for n in range(hp)]


def _diff_prompt(dqk, z, lam, gain, layer, lay):
    B, SEQ, H, NM, RP = lay["B"], lay["SEQ"], lay["DIFF_HEADS"], lay["N_META"], lay["RP"]
    tq = _divisor_tile(SEQ, 256, 16)
    nq = SEQ // tq
    hp = 2 if H % 2 == 0 else 1
    vcol = lay["DV_OFF"] // LANES
    nl = gain.shape[0]
    mrow = RP // NM
    vmem = 8 * hp * SEQ * LANES * 4 + 16 * 1024 * 1024
    in_specs = ([pl.BlockSpec((None,) + lam.shape[1:], lambda b, g, i: (layer, 0, 0)),
                 pl.BlockSpec((nl, LANES), lambda b, g, i: (0, 0))]
                + _head_specs((tq, LANES), hp, lambda b, h, i: (b * nq + i, h))
                + _head_specs((SEQ, LANES), hp, lambda b, h, i: (b, H + h))
                + _head_specs((SEQ, LANES), hp, lambda b, h, i: (b, vcol + h))
                + _head_specs((NM, LANES), hp, lambda b, h, i: (mrow + b, H + h))
                + _head_specs((NM, LANES), hp, lambda b, h, i: (mrow + b, vcol + h)))
    return pl.pallas_call(
        functools.partial(_diff_prompt_kernel, layer=layer, hp=hp),
        grid=(B, H // hp, nq),
        in_specs=in_specs,
        out_specs=pl.BlockSpec((tq, hp * LANES), lambda b, g, i: (b * nq + i, g)),
        out_shape=jax.ShapeDtypeStruct((RP, H * LANES), BF16),
        compiler_params=_params(("parallel", "parallel", "parallel"), vmem),
        name="diff_prompt",
    )(lam, gain, *([dqk] * hp + [dqk] * hp + [z] * hp + [dqk] * hp + [z] * hp))


def _diff_small_kernel(*refs, layer, heads, valid_rows, n_steps, pps):
    if n_steps:
        lam_ref, g_ref, qk_ref, v_ref = refs[1:5]
        kc_refs, vc_refs = refs[5:5 + pps], refs[5 + pps:5 + 2 * pps]
        o_ref, m_ref, l_ref, acc_ref = refs[5 + 2 * pps:]
    else:
        lam_ref, g_ref, qk_ref, v_ref, o_ref, m_ref, l_ref, acc_ref = refs
    T = qk_ref.shape[0]
    d = qk_ref.shape[1] // (4 * heads)
    scale = d ** -0.5
    step = pl.program_id(1)

    def q_map(m):
        return qk_ref[:, m * d:(m + 1) * d].astype(BF16)

    @pl.when(step == 0)
    def _():
        qidx = _iota((T, 1), 0)
        kidx = _iota((1, T), 1)
        visible = (kidx <= qidx) & (kidx < valid_rows)
        for m in range(2 * heads):
            h = m // 2
            km = qk_ref[:, (2 * heads + m) * d:(2 * heads + m + 1) * d].astype(BF16)
            s = jnp.where(visible, _nt_dot(q_map(m), km) * scale, NEG_BIG)
            mx = jnp.max(s, axis=-1, keepdims=True)
            p = jnp.exp(s - mx)
            rows = slice(m * T, (m + 1) * T)
            m_ref[rows, :] = mx
            l_ref[rows, :] = jnp.sum(p, axis=-1, keepdims=True)
            acc_ref[rows, :] = _dot(p.astype(BF16), v_ref[:, h * LANES:(h + 1) * LANES].astype(BF16))

    if n_steps:
        @pl.when(step > 0)
        def _():
            page = kc_refs[0].shape[-1]

            def keys_t(m):
                return jnp.concatenate([r[0, 0, m] for r in kc_refs], axis=1).astype(BF16)

            def values(h):
                return jnp.concatenate([r[0, 0, pl.ds(h, page, stride=heads), :] for r in vc_refs],
                                       axis=0).astype(BF16)

            s = jnp.concatenate([_dot(q_map(m), keys_t(m)) for m in range(2 * heads)],
                                axis=0) * scale
            mx = m_ref[...]
            m_new = jnp.maximum(mx, jnp.max(s, axis=-1, keepdims=True))
            alpha = jnp.exp(mx - m_new)
            p = jnp.exp(s - m_new)
            m_ref[...] = m_new
            l_ref[...] = alpha * l_ref[...] + jnp.sum(p, axis=-1, keepdims=True)
            pb = p.astype(BF16)
            for h in range(heads):
                rows = slice(2 * h * T, (2 * h + 2) * T)
                acc_ref[rows, :] = alpha[rows] * acc_ref[rows, :] + _dot(pb[rows], values(h))

    @pl.when(step == n_steps)
    def _():
        lam, lam_init = _diff_lambda(lam_ref, layer)
        o = acc_ref[...] / l_ref[...]
        g = g_ref[layer:layer + 1, :]
        for h in range(heads):
            oh = o[2 * h * T:(2 * h + 1) * T] - lam * o[(2 * h + 1) * T:(2 * h + 2) * T]
            o_ref[:, h * LANES:(h + 1) * LANES] = _head_norm(oh, g) * (1.0 - lam_init)


def _diff_small(dqk, z, lam, gain, layer, lay, *, row_block, rows, n_seq, valid_rows, cache=None):
    H = lay["DIFF_HEADS"]
    n_pages = lay["N_PAGES"] if cache is not None else 0
    pps = _pages_per_step(n_pages)
    n_steps = n_pages // pps
    nl = gain.shape[0]
    vcol = lay["DV_OFF"] // (H * LANES)

    def ix(f):
        return (lambda b, p, pt: f(b, p, pt)) if n_pages else (lambda b, p: f(b, p, None))

    in_specs = [pl.BlockSpec((None,) + lam.shape[1:], ix(lambda b, p, pt: (layer, 0, 0))),
                pl.BlockSpec((nl, LANES), ix(lambda b, p, pt: (0, 0))),
                pl.BlockSpec((rows, 2 * H * LANES), ix(lambda b, p, pt: (row_block + b, 0))),
                pl.BlockSpec((rows, H * LANES), ix(lambda b, p, pt: (row_block + b, vcol)))]
    args = [lam, gain, dqk, z]
    if n_pages:
        kc, vc, page_table = cache

        def page(i):
            return lambda b, p, pt: (layer, pt[b, (jnp.maximum(p, 1) - 1) * pps + i]) + (0,) * (kc.ndim - 2)

        def vpage(i):
            return lambda b, p, pt: (layer, pt[b, (jnp.maximum(p, 1) - 1) * pps + i], 0, 0)

        in_specs += [pl.BlockSpec((1, 1) + kc.shape[2:], page(i)) for i in range(pps)]
        in_specs += [pl.BlockSpec((1, 1) + vc.shape[2:], vpage(i)) for i in range(pps)]
        args += [kc] * pps + [vc] * pps
    n_rows = 2 * H * rows
    kernel = functools.partial(_diff_small_kernel, layer=layer, heads=H, valid_rows=valid_rows,
                               n_steps=n_steps, pps=pps)
    out_spec = pl.BlockSpec((rows, H * LANES), ix(lambda b, p, pt: (b, 0)))
    scratch = [pltpu.VMEM((n_rows, 1), F32), pltpu.VMEM((n_rows, 1), F32), pltpu.VMEM((n_rows, LANES), F32)]
    out_shape = jax.ShapeDtypeStruct((n_seq * rows, H * LANES), F32)
    params = _params(("parallel", "arbitrary"), 32 * 1024 * 1024)
    if n_pages:
        return pl.pallas_call(
            kernel,
            grid_spec=pltpu.PrefetchScalarGridSpec(
                num_scalar_prefetch=1, grid=(n_seq, n_steps + 1), in_specs=in_specs,
                out_specs=out_spec, scratch_shapes=scratch),
            out_shape=out_shape, compiler_params=params, name="diff_sample",
        )(page_table, *args)
    return pl.pallas_call(
        kernel, grid=(n_seq, 1), in_specs=in_specs, out_specs=out_spec, scratch_shapes=scratch,
        out_shape=out_shape, compiler_params=params, name="diff_meta",
    )(*args)


def _log_sigmoid_pair(z):
    log_beta = jnp.minimum(z, 0.0) - jnp.log(1.0 + jnp.exp(-jnp.abs(z)))
    return log_beta, log_beta - z


def _sb_block(q, k, v, scale, carry, acc, visible, triu):
    z = _nt_dot(q, k) * scale
    log_beta, log_1mb = _log_sigmoid_pair(z)
    if visible is not None:
        log_1mb = jnp.where(visible, log_1mb, 0.0)
    tail = None
    for part in _split_bf16(log_1mb, 2):
        t = _dot(part, triu)
        tail = t if tail is None else tail + t
    w = log_beta + tail + carry
    if visible is not None:
        w = jnp.where(visible, w, NEG_BIG)
    acc = acc + _dot(jnp.exp(w).astype(BF16), v)
    return carry + jnp.sum(log_1mb, axis=-1, keepdims=True), acc


def _triu(n):
    return (_iota((n, n), 0) > _iota((n, n), 1)).astype(BF16)


def _sb_prompt_kernel(g_ref, *refs, layer, hp):
    q_refs, k_refs, v_refs, km_refs, vm_refs = (refs[i * hp:(i + 1) * hp] for i in range(5))
    o_ref, z_s, lb_s, tail_s, rs_s, carry_s, acc_s = refs[5 * hp:]
    qi = pl.program_id(2)
    tq, d = q_refs[0].shape
    scale = d ** -0.5
    qs = [r[...].astype(BF16) for r in q_refs]
    triu = _triu(tq)

    def rows_of(c):
        return pl.ds(pl.multiple_of(c * tq, tq), tq)

    def score_stage(s):
        for h in range(hp):
            z_s[lax.rem(s, 2), h] = _nt_dot(qs[h], k_refs[h][rows_of(qi - s), :].astype(BF16)) * scale

    def weight_stage(s, visible):
        slot = lax.rem(s, 2)
        for h in range(hp):
            log_beta, log_1mb = _log_sigmoid_pair(z_s[slot, h])
            if visible is not None:
                log_1mb = jnp.where(visible, log_1mb, 0.0)
                log_beta = jnp.where(visible, log_beta, NEG_BIG)
            hi, lo = _split_bf16(log_1mb, 2)
            lb_s[slot, h] = log_beta
            tail_s[slot, h] = _dot(hi, triu) + _dot(lo, triu)
            rs_s[slot, h] = jnp.sum(log_1mb, axis=-1, keepdims=True)

    def accumulate_stage(s):
        slot = lax.rem(s, 2)
        for h in range(hp):
            carry = carry_s[h]
            a = jnp.exp(lb_s[slot, h] + tail_s[slot, h] + carry).astype(BF16)
            acc_s[h] += _dot(a, v_refs[h][rows_of(qi - s), :].astype(BF16))
            carry_s[h] = carry + rs_s[slot, h]

    carry_s[...] = jnp.zeros(carry_s.shape, F32)
    acc_s[...] = jnp.zeros(acc_s.shape, F32)
    score_stage(0)

    @pl.when(qi >= 1)
    def _():
        score_stage(1)

    weight_stage(0, _iota((1, tq), 1) < _iota((tq, 1), 0))

    def body(s, _):
        accumulate_stage(s - 2)
        weight_stage(s - 1, None)
        score_stage(s)
        return 0

    lax.fori_loop(2, qi + 1, body, 0)

    @pl.when(qi >= 1)
    def _():
        weight_stage(qi, None)
        accumulate_stage(qi - 1)

    accumulate_stage(qi)
    g = g_ref[layer:layer + 1, :]
    meta_triu = _triu(km_refs[0].shape[0])
    for h in range(hp):
        _, acc = _sb_block(qs[h], km_refs[h][...].astype(BF16), vm_refs[h][...].astype(BF16), scale,
                           carry_s[h], acc_s[h], None, meta_triu)
        o_ref[:, h * LANES:(h + 1) * LANES] = _head_norm(acc, g).astype(o_ref.dtype)


def _sb_prompt(z, gain, layer, lay):
    B, SEQ, H, NM, RP = lay["B"], lay["SEQ"], lay["SB_HEADS"], lay["N_META"], lay["RP"]
    tq = _divisor_tile(SEQ, 256, 16)
    nq = SEQ // tq
    hp = 2 if H % 2 == 0 else 1
    qc, kc, vc = (lay["SQ_OFF"] // LANES, lay["SK_OFF"] // LANES, lay["SV_OFF"] // LANES)
    nl = gain.shape[0]
    mrow = RP // NM
    vmem = 8 * hp * SEQ * LANES * 4 + 24 * 1024 * 1024
    in_specs = ([pl.BlockSpec((nl, LANES), lambda b, g, i: (0, 0))]
                + _head_specs((tq, LANES), hp, lambda b, h, i: (b * nq + i, qc + h))
                + _head_specs((SEQ, LANES), hp, lambda b, h, i: (b, kc + h))
                + _head_specs((SEQ, LANES), hp, lambda b, h, i: (b, vc + h))
                + _head_specs((NM, LANES), hp, lambda b, h, i: (mrow + b, kc + h))
                + _head_specs((NM, LANES), hp, lambda b, h, i: (mrow + b, vc + h)))
    return pl.pallas_call(
        functools.partial(_sb_prompt_kernel, layer=layer, hp=hp),
        grid=(B, H // hp, nq),
        in_specs=in_specs,
        out_specs=pl.BlockSpec((tq, hp * LANES), lambda b, g, i: (b * nq + i, g)),
        out_shape=jax.ShapeDtypeStruct((RP, H * LANES), BF16),
        scratch_shapes=[pltpu.VMEM((2, hp, tq, tq), F32), pltpu.VMEM((2, hp, tq, tq), F32),
                        pltpu.VMEM((2, hp, tq, tq), F32), pltpu.VMEM((2, hp, tq, 1), F32),
                        pltpu.VMEM((hp, tq, 1), F32), pltpu.VMEM((hp, tq, LANES), F32)],
        compiler_params=_params(("parallel", "parallel", "parallel"), vmem),
        name="sb_prompt",
    )(gain, *([z] * (5 * hp)))


def _sb_small_kernel(*refs, layer, heads, pieces, valid_rows, n_steps, pps):
    if n_steps:
        refs = refs[1:]
    g_ref = refs[0]
    q_ref, k_ref, v_ref = (refs[1 + i * pieces:1 + (i + 1) * pieces] for i in range(3))
    rest = refs[1 + 3 * pieces:]
    if n_steps:
        kc_refs, vc_refs = rest[:pps], rest[pps:2 * pps]
        o_ref, carry_ref, acc_ref = rest[2 * pps:]
    else:
        o_ref, carry_ref, acc_ref = rest
    T = o_ref.shape[0]
    d = LANES
    scale = d ** -0.5
    step = pl.program_id(1)
    per_piece = heads // pieces

    def head(piece_refs, h):
        c = (h % per_piece) * d
        return piece_refs[h // per_piece][:, c:c + d].astype(BF16)

    @pl.when(step == 0)
    def _():
        qidx = _iota((T, 1), 0)
        kidx = _iota((1, T), 1)
        visible = (kidx < qidx) & (kidx < valid_rows)
        triu = _triu(T)
        for h in range(heads):
            carry, acc = _sb_block(head(q_ref, h), head(k_ref, h), head(v_ref, h), scale,
                                   jnp.zeros((T, 1), F32), jnp.zeros((T, d), F32), visible, triu)
            carry_ref[h * T:(h + 1) * T, :] = carry
            acc_ref[h * T:(h + 1) * T, :] = acc

    if n_steps:
        @pl.when(step > 0)
        def _():
            page = kc_refs[0].shape[2] // heads

            def rows_of(page_refs, h):
                return jnp.concatenate([r[0, 0, pl.ds(h, page, stride=heads), :] for r in page_refs],
                                       axis=0).astype(BF16)

            z = jnp.concatenate([_nt_dot(head(q_ref, h), rows_of(kc_refs, h)) for h in range(heads)],
                                axis=0) * scale
            log_beta, log_1mb = _log_sigmoid_pair(z)
            triu = _triu(pps * page)
            tail = None
            for part in _split_bf16(log_1mb, 2):
                t = _dot(part, triu)
                tail = t if tail is None else tail + t
            a = jnp.exp(log_beta + tail + carry_ref[...]).astype(BF16)
            carry_ref[...] += jnp.sum(log_1mb, axis=-1, keepdims=True)
            for h in range(heads):
                rows = slice(h * T, (h + 1) * T)
                acc_ref[rows, :] += _dot(a[rows], rows_of(vc_refs, h))

    @pl.when(step == n_steps)
    def _():
        o = _head_norm(acc_ref[...], g_ref[layer:layer + 1, :])
        for h in range(heads):
            o_ref[:, h * d:(h + 1) * d] = o[h * T:(h + 1) * T]


def _sb_small(z, gain, layer, lay, *, row_block, rows, n_seq, valid_rows, cache=None):
    H = lay["SB_HEADS"]
    n_pages = lay["N_PAGES"] if cache is not None else 0
    nl = gain.shape[0]
    width = H * LANES
    cw = math.gcd(lay["SQ_OFF"], width)
    pieces = width // cw

    def ix(f):
        return (lambda b, p, pt: f(b, p, pt)) if n_pages else (lambda b, p: f(b, p, None))

    in_specs = [pl.BlockSpec((nl, LANES), ix(lambda b, p, pt: (0, 0)))]
    in_specs += [pl.BlockSpec((rows, cw), ix((lambda c: lambda b, p, pt: (row_block + b, c))(off // cw + i)))
                 for off in (lay["SQ_OFF"], lay["SK_OFF"], lay["SV_OFF"]) for i in range(pieces)]
    args = [gain] + [z] * (3 * pieces)
    pps = _pages_per_step(n_pages)
    n_steps = n_pages // pps
    if n_pages:
        kc, vc, page_table = cache

        def page(i):
            return lambda b, p, pt: (layer, pt[b, n_pages - jnp.maximum(p, 1) * pps + i], 0, 0)

        in_specs += [pl.BlockSpec((1, 1) + kc.shape[2:], page(i)) for i in range(pps)]
        in_specs += [pl.BlockSpec((1, 1) + vc.shape[2:], page(i)) for i in range(pps)]
        args += [kc] * pps + [vc] * pps
    kernel = functools.partial(_sb_small_kernel, layer=layer, heads=H, pieces=pieces,
                               valid_rows=valid_rows, n_steps=n_steps, pps=pps)
    out_spec = pl.BlockSpec((rows, width), ix(lambda b, p, pt: (b, 0)))
    scratch = [pltpu.VMEM((H * rows, 1), F32), pltpu.VMEM((H * rows, LANES), F32)]
    out_shape = jax.ShapeDtypeStruct((n_seq * rows, width), F32)
    params = _params(("parallel", "arbitrary"), 32 * 1024 * 1024)
    if n_pages:
        return pl.pallas_call(
            kernel,
            grid_spec=pltpu.PrefetchScalarGridSpec(
                num_scalar_prefetch=1, grid=(n_seq, n_steps + 1), in_specs=in_specs,
                out_specs=out_spec, scratch_shapes=scratch),
            out_shape=out_shape, compiler_params=params, name="sb_sample",
        )(page_table, *args)
    return pl.pallas_call(
        kernel, grid=(n_seq, 1), in_specs=in_specs, out_specs=out_spec, scratch_shapes=scratch,
        out_shape=out_shape, compiler_params=params, name="sb_meta",
    )(*args)


def _rope_tables(pos, d_map, rope_dim):
    half = rope_dim // 2
    inv = ROPE_THETA ** (-jnp.arange(half, dtype=F32) / half)
    ang = pos.astype(F32)[:, None] * inv[None, :]
    cos, sin = jnp.cos(ang), jnp.sin(ang)
    rows = pos.shape[0]
    pad = jnp.zeros((rows, d_map - rope_dim), F32)
    zero = jnp.zeros((rows, half), F32)
    c = jnp.concatenate([cos, cos, pad + 1.0], axis=1)
    s1 = jnp.concatenate([-sin, zero, pad], axis=1)
    s2 = jnp.concatenate([zero, sin, pad], axis=1)
    rep = LANES // d_map
    return tuple(jnp.tile(t, (1, rep)) for t in (c, s1, s2))


def kernel(x_prompt, x_sample, cache_diff_k, cache_diff_v, cache_sb_k, cache_sb_v, state_hgrn,
           page_table, meta_tokens, norm_mix, norm_ffn, norm_final, w_in, hg_lower_bounds, hg_norm,
           diff_lambda, diff_subln, sb_norm, w_out, w_gate_up, w_down):
    B, SEQ, D = x_prompt.shape
    DB, DEC_SEQ, _ = x_sample.shape
    DEPTH = w_in.shape[0]
    NM = meta_tokens.shape[0]
    HGH = state_hgrn.shape[2]
    DH = cache_diff_v.shape[3]
    DQK = cache_diff_k.shape[4]
    SH = cache_sb_k.shape[3]
    PAGE = cache_sb_k.shape[2]
    NP = page_table.shape[1]
    assert DEC_SEQ <= SAMPLE_ROWS and NM % 16 == 0 and SEQ % NM == 0 and 2 * DQK == LANES
    assert state_hgrn.shape[3:] == (LANES, LANES) and HGH == DH
    HGW, DW, SW = HGH * LANES, DH * LANES, SH * LANES
    RP, RM, RS = B * SEQ, B * NM, DB * SAMPLE_ROWS
    lay = dict(B=B, SEQ=SEQ, DB=DB, DEC_SEQ=DEC_SEQ, N_META=NM, HG_HEADS=HGH, DIFF_HEADS=DH, SB_HEADS=SH,
               N_PAGES=NP, RP=RP, RM=RM, RS=RS,
               DV_OFF=4 * HGW + 2 * DW, SQ_OFF=4 * HGW + 3 * DW, SK_OFF=4 * HGW + 3 * DW + SW,
               SV_OFF=4 * HGW + 3 * DW + 2 * SW)

    sample = jnp.pad(x_sample, ((0, 0), (0, SAMPLE_ROWS - DEC_SEQ), (0, 0)))
    x = jnp.concatenate([x_prompt.reshape(RP, D),
                         jnp.broadcast_to(meta_tokens[None], (B, NM, D)).reshape(RM, D),
                         sample.reshape(RS, D)], axis=0)
    pos = jnp.concatenate([jnp.tile(NM + jnp.arange(SEQ, dtype=jnp.int32), B),
                           jnp.tile(jnp.arange(NM, dtype=jnp.int32), B),
                           jnp.tile(NP * PAGE + jnp.arange(SAMPLE_ROWS, dtype=jnp.int32), DB)])
    tables = _rope_tables(pos, DQK, DQK // 4)

    ck_d = jnp.transpose(cache_diff_k, (0, 1, 3, 4, 2))
    cv_d = cache_diff_v.reshape(cache_diff_v.shape[:2] + (PAGE * DH, LANES))
    ck_s = cache_sb_k.reshape(cache_sb_k.shape[:2] + (PAGE * SH, LANES))
    cv_s = cache_sb_v.reshape(cache_sb_v.shape[:2] + (PAGE * SH, LANES))

    w_in_b, w_out_b, w_gu_b, w_dn_b = w_in, w_out, w_gate_up, w_down
    g_mix = norm_mix.reshape(DEPTH, 1, D)
    g_ffn = norm_ffn.reshape(DEPTH, 1, D)
    g_fin = norm_final.reshape(1, 1, D)
    meta_blk = RP // NM
    samp_blk = (RP + RM) // SAMPLE_ROWS
    small_meta = dict(row_block=meta_blk, rows=NM, n_seq=B, valid_rows=NM)
    small_samp = dict(row_block=samp_blk, rows=SAMPLE_ROWS, n_seq=DB, valid_rows=DEC_SEQ)

    groups = (("k_d", 2 * DH, DQK), ("v_d", DH, LANES), ("k_s", SH, LANES), ("v_s", SH, LANES))
    new_p = {n: jnp.zeros((DEPTH, B, NM + SEQ, h, d), F32) for n, h, d in groups}
    new_s = {n: jnp.zeros((DEPTH, DB, DEC_SEQ, h, d), F32) for n, h, d in groups}
    states_p, states_s = [], []

    def store_rows(l, name, a, heads):
        new_p[name] = new_p[name].at[l, :, :NM].set(a[RP:RP + RM].reshape(B, NM, heads, -1))
        new_p[name] = new_p[name].at[l, :, NM:].set(a[:RP].reshape(B, SEQ, heads, -1))
        new_s[name] = new_s[name].at[l].set(a[RP + RM:].reshape(DB, SAMPLE_ROWS, heads, -1)[:, :DEC_SEQ])

    for l in range(DEPTH):
        hn = _rmsnorm(x, g_mix, l, BF16)
        z = _matmul(hn, w_in_b, l)
        dqk = _rope(z, 2, 2 * DW, tables)

        hg_m, hg_p, s_p = _hgrn_prompt(z, hg_lower_bounds, hg_norm, l, lay)
        hg_s, s_s = _hgrn_sample(z, state_hgrn, hg_lower_bounds, hg_norm, l, lay)
        df_p = _diff_prompt(dqk, z, diff_lambda, diff_subln, l, lay)
        df_m = _diff_small(dqk, z, diff_lambda, diff_subln, l, lay, **small_meta)
        df_s = _diff_small(dqk, z, diff_lambda, diff_subln, l, lay, **small_samp,
                           cache=(ck_d, cv_d, page_table))
        sb_p = _sb_prompt(z, sb_norm, l, lay)
        sb_m = _sb_small(z, sb_norm, l, lay, **small_meta)
        sb_s = _sb_small(z, sb_norm, l, lay, **small_samp, cache=(ck_s, cv_s, page_table))

        small = jnp.concatenate([
            jnp.concatenate([hg_m, df_m.astype(BF16), sb_m.astype(BF16)], axis=1),
            jnp.concatenate([hg_s.astype(BF16), df_s.astype(BF16), sb_s.astype(BF16)], axis=1)], axis=0)
        mix = jnp.zeros((RP + RM + RS, D), BF16)
        mix = mix.at[:RP, :HGW].set(hg_p).at[:RP, HGW:HGW + DW].set(df_p).at[:RP, HGW + DW:].set(sb_p)
        mix = mix.at[RP:].set(small)
        x = _matmul(mix, w_out_b, l, res=x)
        h = _ffn_up(_rmsnorm(x, g_ffn, l, BF16), w_gu_b, l)
        x = _matmul(h, w_dn_b, l, res=x)

        store_rows(l, "k_d", dqk[:, DW:], 2 * DH)
        store_rows(l, "v_d", z[:, lay["DV_OFF"]:lay["DV_OFF"] + DW], DH)
        store_rows(l, "k_s", z[:, lay["SK_OFF"]:lay["SK_OFF"] + SW], SH)
        store_rows(l, "v_s", z[:, lay["SV_OFF"]:lay["SV_OFF"] + SW], SH)
        states_p.append(s_p)
        states_s.append(s_s)

    y = _rmsnorm(x, g_fin, 0, F32)
    y_prompt = y[:RP].reshape(B, SEQ, D)
    y_sample = y[RP + RM:].reshape(DB, SAMPLE_ROWS, D)[:, :DEC_SEQ]
    return (y_prompt, y_sample, new_p["k_d"], new_p["v_d"], new_p["k_s"], new_p["v_s"], jnp.stack(states_p),
            new_s["k_d"], new_s["v_d"], new_s["k_s"], new_s["v_s"], jnp.stack(states_s))
```

```python
import functools
import math

import jax
import jax.numpy as jnp
from jax import lax
from jax.experimental import pallas as pl
from jax.experimental.pallas import tpu as pltpu

F32 = jnp.float32
BF16 = jnp.bfloat16
EPS = 1e-6
NEG_BIG = -1e30
ROPE_THETA = 500000.0
LANES = 128
SAMPLE_ROWS = 8
V7X_VMEM_BYTES = 64 * 1024 * 1024
VMEM_CAP = V7X_VMEM_BYTES - 8 * 1024 * 1024


def _params(semantics, vmem_bytes):
    limit = int(min(max(vmem_bytes, 16 * 1024 * 1024), VMEM_CAP))
    return pltpu.CompilerParams(dimension_semantics=semantics, vmem_limit_bytes=limit)


def _divisor_tile(n, target, multiple):
    best = None
    for t in range(multiple, min(n, target) + 1, multiple):
        if n % t == 0:
            best = t
    return best if best is not None else n


def _pages_per_step(n_pages, most):
    for pps in (4, 2):
        if pps <= most and n_pages > 0 and n_pages % pps == 0:
            return pps
    return 1


def _nt_dot(a, b):
    return lax.dot_general(a, b, (((1,), (1,)), ((), ())), preferred_element_type=F32)


def _dot(a, b):
    return jnp.dot(a, b, preferred_element_type=F32)


def _iota(shape, dim):
    return lax.broadcasted_iota(jnp.int32, shape, dim)


def _split_bf16(x, parts):
    out = []
    r = x
    for i in range(parts):
        p = r.astype(BF16)
        out.append(p)
        if i + 1 < parts:
            r = r - p.astype(F32)
    return out


def _sigmoid_pair(z):
    e = jnp.exp(-jnp.abs(z))
    r = 1.0 / (1.0 + e)
    er = e * r
    pos = z >= 0
    return jnp.where(pos, r, er), jnp.where(pos, er, r)


def _silu(z):
    return z * _sigmoid_pair(z)[0]


def _head_norm(o, g):
    return o * lax.rsqrt(jnp.mean(o * o, axis=-1, keepdims=True) + EPS) * g


def _rmsnorm_kernel(x_ref, g_ref, o_ref):
    x = x_ref[...]
    y = x * lax.rsqrt(jnp.mean(x * x, axis=-1, keepdims=True) + EPS)
    o_ref[...] = (y * g_ref[...]).astype(o_ref.dtype)


def _rmsnorm(x, gains, layer, out_dtype):
    M, D = x.shape
    tr = _divisor_tile(M, 256, 16)
    vmem = 2 * tr * D * (4 + jnp.dtype(out_dtype).itemsize) + 3 * tr * D * 4
    return pl.pallas_call(
        _rmsnorm_kernel,
        grid=(M // tr,),
        in_specs=[pl.BlockSpec((tr, D), lambda i: (i, 0)),
                  pl.BlockSpec((None, 1, D), lambda i: (layer, 0, 0))],
        out_specs=pl.BlockSpec((tr, D), lambda i: (i, 0)),
        out_shape=jax.ShapeDtypeStruct((M, D), out_dtype),
        compiler_params=_params(("parallel",), vmem),
        name="rmsnorm",
    )(x, gains)


def _rope_kernel(x_ref, c_ref, s1_ref, s2_ref, o_ref):
    c = c_ref[...]
    s1 = s1_ref[...]
    s2 = s2_ref[...]
    for j in range(x_ref.shape[1] // LANES):
        x = x_ref[:, j * LANES:(j + 1) * LANES]
        up = pltpu.roll(x, LANES - 8, 1)
        dn = pltpu.roll(x, 8, 1)
        o_ref[:, j * LANES:(j + 1) * LANES] = x * c + up * s1 + dn * s2


def _rope(z, col_block, width, tables):
    M = z.shape[0]
    tr = _divisor_tile(M, 512, 8)
    tab = pl.BlockSpec((tr, LANES), lambda i: (i, 0))
    vmem = 4 * tr * width * 4 + 6 * tr * LANES * 4 + 4 * tr * LANES * 4
    return pl.pallas_call(
        _rope_kernel,
        grid=(M // tr,),
        in_specs=[pl.BlockSpec((tr, width), lambda i: (i, col_block)), tab, tab, tab],
        out_specs=pl.BlockSpec((tr, width), lambda i: (i, 0)),
        out_shape=jax.ShapeDtypeStruct((M, width), F32),
        compiler_params=_params(("parallel",), vmem),
        name="rope",
    )(z, *tables)


def _matmul_kernel(a_ref, w_ref, *rest, nk, has_res):
    o_ref = rest[-1]
    p = _dot(a_ref[...], w_ref[...].astype(BF16))
    if has_res:
        first = p + rest[0][...]
    else:
        first = p
    if nk == 1:
        o_ref[...] = first
    else:
        k = pl.program_id(2)

        @pl.when(k == 0)
        def _():
            o_ref[...] = first

        @pl.when(k > 0)
        def _():
            o_ref[...] += p


def _matmul_tiles(M, K, N):
    wide = K > 4096
    bm = _divisor_tile(M, 832 if wide else 1664, 16)
    bn = _divisor_tile(N, 256 if wide else 512, LANES)
    return bm, K, bn


def _matmul(a, w, layer, res=None):
    M, K = a.shape
    N = w.shape[2]
    bm, bk, bn = _matmul_tiles(M, K, N)
    nk = K // bk
    a_bufs = 1 if nk == 1 else 2
    a_kw = dict(pipeline_mode=pl.Buffered(1)) if nk == 1 else {}
    in_specs = [pl.BlockSpec((bm, bk), lambda i, j, k: (i, k), **a_kw),
                pl.BlockSpec((None, bk, bn), lambda i, j, k: (layer, k, j))]
    args = [a, w]
    if res is not None:
        in_specs.append(pl.BlockSpec((bm, bn), lambda i, j, k: (i, j)))
        args.append(res)
    vmem = (a_bufs * bm * bk * 2 + 2 * bk * bn * w.dtype.itemsize + bk * bn * 2
            + (4 if res is not None else 2) * bm * bn * 4 + 2 * bm * bn * 4)
    return pl.pallas_call(
        functools.partial(_matmul_kernel, nk=nk, has_res=res is not None),
        grid=(M // bm, N // bn, nk),
        in_specs=in_specs,
        out_specs=pl.BlockSpec((bm, bn), lambda i, j, k: (i, j)),
        out_shape=jax.ShapeDtypeStruct((M, N), F32),
        compiler_params=_params(("parallel", "parallel", "arbitrary"), vmem),
        name="matmul",
    )(*args)


def _ffn_up_kernel(a_ref, wg_ref, wu_ref, o_ref):
    a = a_ref[...]
    g = _dot(a, wg_ref[...].astype(BF16))
    u = _dot(a, wu_ref[...].astype(BF16))
    o_ref[...] = (_silu(g) * u).astype(o_ref.dtype)


def _ffn_up(a, w_gu, layer):
    M, K = a.shape
    F = w_gu.shape[2] // 2
    bm = _divisor_tile(M, 1664, 16)
    bn = _divisor_tile(F, 256, LANES)
    nf = F // bn
    vmem = (bm * K * 2 + 4 * K * bn * w_gu.dtype.itemsize + 2 * K * bn * 2 + 2 * bm * bn * 2
            + 4 * bm * bn * 4)
    return pl.pallas_call(
        _ffn_up_kernel,
        grid=(M // bm, nf),
        in_specs=[pl.BlockSpec((bm, K), lambda i, j: (i, 0), pipeline_mode=pl.Buffered(1)),
                  pl.BlockSpec((None, K, bn), lambda i, j: (layer, 0, j)),
                  pl.BlockSpec((None, K, bn), lambda i, j: (layer, 0, nf + j))],
        out_specs=pl.BlockSpec((bm, bn), lambda i, j: (i, j)),
        out_shape=jax.ShapeDtypeStruct((M, F), BF16),
        compiler_params=_params(("parallel", "parallel"), vmem),
        name="ffn_up",
    )(a, w_gu, w_gu)


def _lower_bound(lbraw_ref, layer):
    raw = lbraw_ref[...]
    e = jnp.exp(raw - jnp.max(raw, axis=0, keepdims=True))
    total = jnp.sum(e, axis=0, keepdims=True)
    if layer == 0:
        return jnp.zeros_like(total)
    return jnp.sum(e[1:layer + 1], axis=0, keepdims=True) / total


def _gla_chunk(st, hq, hf, hi, lb, *, valid_rows, sub):
    C, K = hq.shape
    sig, sig_neg = _sigmoid_pair(hf)
    q = _silu(hq) * (K ** -0.5)
    gl = jnp.log(lb + (1.0 - lb) * sig)
    k = (1.0 - lb) * sig_neg
    v = hi
    if valid_rows < C:
        ok = _iota((C, 1), 0) < valid_rows
        gl = jnp.where(ok, gl, 0.0)
        k = jnp.where(ok, k, 0.0)

    row = _iota((C, C), 0)
    col = _iota((C, C), 1)
    tril = (col <= row).astype(BF16)
    G = None
    for part in _split_bf16(gl, 3):
        t = _dot(tril, part)
        G = t if G is None else G + t

    o = _nt_dot((q * jnp.exp(G)).astype(BF16), st.astype(BF16))

    A = jnp.zeros((C, C), F32)
    b = sub
    while b < C:
        nb = C // b
        ends = [G[r * b + b - 1:r * b + b] for r in range(nb)]
        e_cur = jnp.concatenate([jnp.broadcast_to(e, (b, K)) for e in ends], axis=0)
        e_prev = jnp.concatenate([jnp.zeros((b, K), F32)]
                                 + [jnp.broadcast_to(e, (b, K)) for e in ends[:-1]], axis=0)
        qb = (q * jnp.exp(G - e_prev)).astype(BF16)
        kb = (k * jnp.exp(e_cur - G)).astype(BF16)
        sh = int(math.log2(b))
        in_level = (((row >> (sh + 1)) == (col >> (sh + 1)))
                    & (((row >> sh) & 1) == 1) & (((col >> sh) & 1) == 0))
        A = jnp.where(in_level, _nt_dot(qb, kb), A)
        b *= 2

    lane = _iota((sub, C), 1)
    trow = _iota((sub, 1), 0)
    blocks = []
    for i in range(C // sub):
        Gi = G[i * sub:(i + 1) * sub]
        qi = q[i * sub:(i + 1) * sub]
        Ai = A[i * sub:(i + 1) * sub]
        for s in range(sub):
            r = i * sub + s
            expo = jnp.where(trow >= s, Gi - G[r:r + 1], NEG_BIG)
            colv = jnp.sum(qi * k[r:r + 1] * jnp.exp(expo), axis=-1, keepdims=True)
            Ai = jnp.where(lane == r, colv, Ai)
        blocks.append(Ai)
    A = jnp.concatenate(blocks, axis=0) if len(blocks) > 1 else blocks[0]
    o = o + _dot(A.astype(BF16), v.astype(BF16))

    g_last = G[C - 1:C]
    k_dec = (k * jnp.exp(g_last - G)).astype(BF16)
    st_new = st * jnp.exp(g_last) + lax.dot_general(
        v.astype(BF16), k_dec, (((0,), (0,)), ((), ())), preferred_element_type=F32)
    return st_new, o


def _hgrn_prompt_kernel(lbraw_ref, gn_ref, mq_ref, mf_ref, mi_ref, mg_ref,
                        hq_ref, hf_ref, hi_ref, hg_ref, om_ref, op_ref, s_ref, *, layer, chunk, hp):
    gn = gn_ref[layer:layer + 1, :]
    n_meta = mq_ref.shape[0]
    K = LANES
    lanes = [slice(h * K, (h + 1) * K) for h in range(hp)]
    lb_all = _lower_bound(lbraw_ref, layer)
    lbs = [lb_all[:, c] for c in lanes]

    states = []
    for h, c in enumerate(lanes):
        st, o = _gla_chunk(jnp.zeros((K, K), F32), mq_ref[:, c], mf_ref[:, c], mi_ref[:, c], lbs[h],
                           valid_rows=n_meta, sub=min(16, n_meta))
        om_ref[:, c] = (_head_norm(o, gn) * _silu(mg_ref[:, c])).astype(om_ref.dtype)
        states.append(st)

    def body(ci, states):
        rows = pl.ds(pl.multiple_of(ci * chunk, chunk), chunk)
        out = []
        for h, c in enumerate(lanes):
            st, o = _gla_chunk(states[h], hq_ref[rows, c], hf_ref[rows, c], hi_ref[rows, c], lbs[h],
                               valid_rows=chunk, sub=16)
            op_ref[rows, c] = (_head_norm(o, gn) * _silu(hg_ref[rows, c])).astype(op_ref.dtype)
            out.append(st)
        return tuple(out)

    states = lax.fori_loop(0, hq_ref.shape[0] // chunk, body, tuple(states))
    for h in range(hp):
        s_ref[h] = states[h].T


def _hgrn_prompt(z, lb_raw, gnorm, layer, lay):
    B, SEQ, H, NM, RP, RM = lay["B"], lay["SEQ"], lay["HG_HEADS"], lay["N_META"], lay["RP"], lay["RM"]
    chunk = _divisor_tile(SEQ, 128, 16)
    nl = lb_raw.shape[0]
    hp = 2 if H % 2 == 0 else 1
    G, W = H // hp, hp * LANES

    def col(k):
        return lambda b, g: (b, k * G + g)

    def mcol(k):
        return lambda b, g: (RP // NM + b, k * G + g)

    prompt = [pl.BlockSpec((SEQ, W), col(k)) for k in range(4)]
    meta = [pl.BlockSpec((NM, W), mcol(k)) for k in range(4)]
    vmem = 16 * SEQ * W * 4 + 8 * 1024 * 1024
    return pl.pallas_call(
        functools.partial(_hgrn_prompt_kernel, layer=layer, chunk=chunk, hp=hp),
        grid=(B, G),
        in_specs=[pl.BlockSpec((nl, W), lambda b, g: (0, g)),
                  pl.BlockSpec((nl, LANES), lambda b, g: (0, 0))] + meta + prompt,
        out_specs=[pl.BlockSpec((NM, W), lambda b, g: (b, g)),
                   pl.BlockSpec((SEQ, W), lambda b, g: (b, g)),
                   pl.BlockSpec((None, hp, LANES, LANES), lambda b, g: (b, g, 0, 0))],
        out_shape=[jax.ShapeDtypeStruct((RM, H * LANES), BF16),
                   jax.ShapeDtypeStruct((RP, H * LANES), BF16),
                   jax.ShapeDtypeStruct((B, H, LANES, LANES), F32)],
        compiler_params=_params(("parallel", "parallel"), vmem),
        name="hgrn_prompt",
    )(lb_raw, gnorm, z, z, z, z, z, z, z, z)


def _hgrn_sample_kernel(lbraw_ref, gn_ref, s0_ref, hq_ref, hf_ref, hi_ref, hg_ref, o_ref, s_ref,
                        *, layer, valid_rows):
    lb = _lower_bound(lbraw_ref, layer)
    gn = gn_ref[layer:layer + 1, :]
    st, o = _gla_chunk(s0_ref[...].T, hq_ref[...], hf_ref[...], hi_ref[...], lb,
                       valid_rows=valid_rows, sub=hq_ref.shape[0])
    o_ref[...] = _head_norm(o, gn) * _silu(hg_ref[...])
    s_ref[...] = st.T


def _hgrn_sample(z, state, lb_raw, gnorm, layer, lay):
    DB, H, RS = lay["DB"], lay["HG_HEADS"], lay["RS"]
    base = (lay["RP"] + lay["RM"]) // SAMPLE_ROWS
    nl = lb_raw.shape[0]
    rows = [pl.BlockSpec((SAMPLE_ROWS, LANES), (lambda k: lambda b, h: (base + b, k * H + h))(k))
            for k in range(4)]
    return pl.pallas_call(
        functools.partial(_hgrn_sample_kernel, layer=layer, valid_rows=lay["DEC_SEQ"]),
        grid=(DB, H),
        in_specs=[pl.BlockSpec((nl, LANES), lambda b, h: (0, h)),
                  pl.BlockSpec((nl, LANES), lambda b, h: (0, 0)),
                  pl.BlockSpec((None, None, None, LANES, LANES), lambda b, h: (layer, b, h, 0, 0))] + rows,
        out_specs=[pl.BlockSpec((SAMPLE_ROWS, LANES), lambda b, h: (b, h)),
                   pl.BlockSpec((None, None, LANES, LANES), lambda b, h: (b, h, 0, 0))],
        out_shape=[jax.ShapeDtypeStruct((RS, H * LANES), F32),
                   jax.ShapeDtypeStruct((DB, H, LANES, LANES), F32)],
        compiler_params=_params(("parallel", "parallel"), 16 * 1024 * 1024),
        name="hgrn_sample",
    )(lb_raw, gnorm, state, z, z, z, z)


def _diff_lambda(lam_ref, layer):
    lp = lam_ref[...]
    a = jnp.sum(lp[0:1] * lp[1:2], axis=-1, keepdims=True)
    b = jnp.sum(lp[2:3] * lp[3:4], axis=-1, keepdims=True)
    lam_init = 0.8 - 0.6 * math.exp(-0.3 * layer)
    return jnp.exp(a) - jnp.exp(b) + lam_init, lam_init


def _softmax_step(carry, s, v):
    mx, l, acc = carry
    m_new = jnp.maximum(mx, jnp.max(s, axis=-1, keepdims=True))
    alpha = jnp.exp(mx - m_new)
    p = jnp.exp(s - m_new)
    return (m_new, alpha * l + jnp.sum(p, axis=-1, keepdims=True),
            alpha * acc + _dot(p.astype(BF16), v))


def _diff_prompt_kernel(lam_ref, g_ref, *refs, layer, hp):
    q_refs, k_refs, v_refs, km_refs, vm_refs = (refs[i * hp:(i + 1) * hp] for i in range(5))
    o_ref = refs[5 * hp]
    qi = pl.program_id(2)
    tq, width = q_refs[0].shape
    d = width // 2
    scale = d ** -0.5
    lam, lam_init = _diff_lambda(lam_ref, layer)
    g = g_ref[layer:layer + 1, :]
    qs = [[r[:, m * d:(m + 1) * d].astype(BF16) for m in range(2)] for r in q_refs]

    def scores(h, k):
        return jnp.concatenate(
            [_nt_dot(qs[h][m], k[:, m * d:(m + 1) * d].astype(BF16)) for m in range(2)], axis=0) * scale

    state = []
    for h in range(hp):
        s = scores(h, km_refs[h][...])
        mx = jnp.max(s, axis=-1, keepdims=True)
        p = jnp.exp(s - mx)
        state.append((mx, jnp.sum(p, axis=-1, keepdims=True),
                      _dot(p.astype(BF16), vm_refs[h][...].astype(BF16))))

    diag = pl.ds(pl.multiple_of(qi * tq, tq), tq)
    causal = _iota((1, tq), 1) <= _iota((tq, 1), 0)
    causal = jnp.concatenate([causal, causal], axis=0)
    for h in range(hp):
        s = jnp.where(causal, scores(h, k_refs[h][diag, :]), NEG_BIG)
        state[h] = _softmax_step(state[h], s, v_refs[h][diag, :].astype(BF16))

    def body(j, state):
        rows = pl.ds(pl.multiple_of(j * tq, tq), tq)
        return tuple(_softmax_step(state[h], scores(h, k_refs[h][rows, :]), v_refs[h][rows, :].astype(BF16))
                     for h in range(hp))

    state = lax.fori_loop(0, qi, body, tuple(state))
    for h in range(hp):
        _, l, acc = state[h]
        o = acc / l
        o = o[:tq] - lam * o[tq:]
        o_ref[:, h * LANES:(h + 1) * LANES] = (_head_norm(o, g) * (1.0 - lam_init)).astype(o_ref.dtype)


def _head_specs(shape, hp, index):
    return [pl.BlockSpec(shape, (lambda n: lambda b, g, i: index(b, g * hp + n, i))(n)) for n in range(hp)]


def _diff_prompt(dqk, z, lam, gain, layer, lay):
    B, SEQ, H, NM, RP = lay["B"], lay["SEQ"], lay["DIFF_HEADS"], lay["N_META"], lay["RP"]
    tq = _divisor_tile(SEQ, 256, 16)
    nq = SEQ // tq
    hp = 2 if H % 2 == 0 else 1
    vcol = lay["DV_OFF"] // LANES
    nl = gain.shape[0]
    mrow = RP // NM
    vmem = 8 * hp * SEQ * LANES * 4 + 16 * 1024 * 1024
    in_specs = ([pl.BlockSpec((None,) + lam.shape[1:], lambda b, g, i: (layer, 0, 0)),
                 pl.BlockSpec((nl, LANES), lambda b, g, i: (0, 0))]
                + _head_specs((tq, LANES), hp, lambda b, h, i: (b * nq + i, h))
                + _head_specs((SEQ, LANES), hp, lambda b, h, i: (b, H + h))
                + _head_specs((SEQ, LANES), hp, lambda b, h, i: (b, vcol + h))
                + _head_specs((NM, LANES), hp, lambda b, h, i: (mrow + b, H + h))
                + _head_specs((NM, LANES), hp, lambda b, h, i: (mrow + b, vcol + h)))
    return pl.pallas_call(
        functools.partial(_diff_prompt_kernel, layer=layer, hp=hp),
        grid=(B, H // hp, nq),
        in_specs=in_specs,
        out_specs=pl.BlockSpec((tq, hp * LANES), lambda b, g, i: (b * nq + i, g)),
        out_shape=jax.ShapeDtypeStruct((RP, H * LANES), BF16),
        compiler_params=_params(("parallel", "parallel", "parallel"), vmem),
        name="diff_prompt",
    )(lam, gain, *([dqk] * hp + [dqk] * hp + [z] * hp + [dqk] * hp + [z] * hp))


def _diff_small_kernel(*refs, layer, heads, valid_rows, n_steps, pps):
    if n_steps:
        lam_ref, g_ref, qk_ref, v_ref = refs[1:5]
        kc_refs, vc_refs = refs[5:5 + pps], refs[5 + pps:5 + 2 * pps]
        o_ref, m_ref, l_ref, acc_ref = refs[5 + 2 * pps:]
    else:
        lam_ref, g_ref, qk_ref, v_ref, o_ref, m_ref, l_ref, acc_ref = refs
    T = qk_ref.shape[0]
    d = qk_ref.shape[1] // (4 * heads)
    scale = d ** -0.5
    step = pl.program_id(1)

    def q_map(m):
        return qk_ref[:, m * d:(m + 1) * d].astype(BF16)

    @pl.when(step == 0)
    def _():
        qidx = _iota((T, 1), 0)
        kidx = _iota((1, T), 1)
        visible = (kidx <= qidx) & (kidx < valid_rows)
        for m in range(2 * heads):
            h = m // 2
            km = qk_ref[:, (2 * heads + m) * d:(2 * heads + m + 1) * d].astype(BF16)
            s = jnp.where(visible, _nt_dot(q_map(m), km) * scale, NEG_BIG)
            mx = jnp.max(s, axis=-1, keepdims=True)
            p = jnp.exp(s - mx)
            rows = slice(m * T, (m + 1) * T)
            m_ref[rows, :] = mx
            l_ref[rows, :] = jnp.sum(p, axis=-1, keepdims=True)
            acc_ref[rows, :] = _dot(p.astype(BF16), v_ref[:, h * LANES:(h + 1) * LANES].astype(BF16))

    if n_steps:
        @pl.when(step > 0)
        def _():
            page = kc_refs[0].shape[-1]

            def keys_t(m):
                return jnp.concatenate([r[0, 0, m] for r in kc_refs], axis=1).astype(BF16)

            def values(h):
                return jnp.concatenate([r[0, 0, pl.ds(h, page, stride=heads), :] for r in vc_refs],
                                       axis=0).astype(BF16)

            s = jnp.concatenate([_dot(q_map(m), keys_t(m)) for m in range(2 * heads)],
                                axis=0) * scale
            mx = m_ref[...]
            m_new = jnp.maximum(mx, jnp.max(s, axis=-1, keepdims=True))
            alpha = jnp.exp(mx - m_new)
            p = jnp.exp(s - m_new)
            m_ref[...] = m_new
            l_ref[...] = alpha * l_ref[...] + jnp.sum(p, axis=-1, keepdims=True)
            pb = p.astype(BF16)
            for h in range(heads):
                rows = slice(2 * h * T, (2 * h + 2) * T)
                acc_ref[rows, :] = alpha[rows] * acc_ref[rows, :] + _dot(pb[rows], values(h))

    @pl.when(step == n_steps)
    def _():
        lam, lam_init = _diff_lambda(lam_ref, layer)
        o = acc_ref[...] / l_ref[...]
        g = g_ref[layer:layer + 1, :]
        for h in range(heads):
            oh = o[2 * h * T:(2 * h + 1) * T] - lam * o[(2 * h + 1) * T:(2 * h + 2) * T]
            o_ref[:, h * LANES:(h + 1) * LANES] = _head_norm(oh, g) * (1.0 - lam_init)


def _diff_small(dqk, z, lam, gain, layer, lay, *, row_block, rows, n_seq, valid_rows, cache=None):
    H = lay["DIFF_HEADS"]
    n_pages = lay["N_PAGES"] if cache is not None else 0
    pps = _pages_per_step(n_pages, 4)
    n_steps = n_pages // pps
    nl = gain.shape[0]
    vcol = lay["DV_OFF"] // (H * LANES)

    def ix(f):
        return (lambda b, p, pt: f(b, p, pt)) if n_pages else (lambda b, p: f(b, p, None))

    in_specs = [pl.BlockSpec((None,) + lam.shape[1:], ix(lambda b, p, pt: (layer, 0, 0))),
                pl.BlockSpec((nl, LANES), ix(lambda b, p, pt: (0, 0))),
                pl.BlockSpec((rows, 2 * H * LANES), ix(lambda b, p, pt: (row_block + b, 0))),
                pl.BlockSpec((rows, H * LANES), ix(lambda b, p, pt: (row_block + b, vcol)))]
    args = [lam, gain, dqk, z]
    if n_pages:
        kc, vc, page_table = cache

        def page(i):
            return lambda b, p, pt: (layer, pt[b, (jnp.maximum(p, 1) - 1) * pps + i]) + (0,) * (kc.ndim - 2)

        def vpage(i):
            return lambda b, p, pt: (layer, pt[b, (jnp.maximum(p, 1) - 1) * pps + i], 0, 0)

        in_specs += [pl.BlockSpec((1, 1) + kc.shape[2:], page(i)) for i in range(pps)]
        in_specs += [pl.BlockSpec((1, 1) + vc.shape[2:], vpage(i)) for i in range(pps)]
        args += [kc] * pps + [vc] * pps
    n_rows = 2 * H * rows
    kernel = functools.partial(_diff_small_kernel, layer=layer, heads=H, valid_rows=valid_rows,
                               n_steps=n_steps, pps=pps)
    out_spec = pl.BlockSpec((rows, H * LANES), ix(lambda b, p, pt: (b, 0)))
    scratch = [pltpu.VMEM((n_rows, 1), F32), pltpu.VMEM((n_rows, 1), F32), pltpu.VMEM((n_rows, LANES), F32)]
    out_shape = jax.ShapeDtypeStruct((n_seq * rows, H * LANES), F32)
    params = _params(("parallel", "arbitrary"), 32 * 1024 * 1024)
    if n_pages:
        return pl.pallas_call(
            kernel,
            grid_spec=pltpu.PrefetchScalarGridSpec(
                num_scalar_prefetch=1, grid=(n_seq, n_steps + 1), in_specs=in_specs,
                out_specs=out_spec, scratch_shapes=scratch),
            out_shape=out_shape, compiler_params=params, name="diff_sample",
        )(page_table, *args)
    return pl.pallas_call(
        kernel, grid=(n_seq, 1), in_specs=in_specs, out_specs=out_spec, scratch_shapes=scratch,
        out_shape=out_shape, compiler_params=params, name="diff_meta",
    )(*args)


def _log_sigmoid_pair(z):
    log_beta = jnp.minimum(z, 0.0) - jnp.log(1.0 + jnp.exp(-jnp.abs(z)))
    return log_beta, log_beta - z


def _sb_block(q, k, v, scale, carry, acc, visible, triu):
    z = _nt_dot(q, k) * scale
    log_beta, log_1mb = _log_sigmoid_pair(z)
    if visible is not None:
        log_1mb = jnp.where(visible, log_1mb, 0.0)
    tail = None
    for part in _split_bf16(log_1mb, 2):
        t = _dot(part, triu)
        tail = t if tail is None else tail + t
    w = log_beta + tail + carry
    if visible is not None:
        w = jnp.where(visible, w, NEG_BIG)
    acc = acc + _dot(jnp.exp(w).astype(BF16), v)
    return carry + jnp.sum(log_1mb, axis=-1, keepdims=True), acc


def _triu(n):
    return (_iota((n, n), 0) > _iota((n, n), 1)).astype(BF16)


def _sb_prompt_kernel(g_ref, *refs, layer, hp):
    q_refs, k_refs, v_refs, km_refs, vm_refs = (refs[i * hp:(i + 1) * hp] for i in range(5))
    o_ref, z_s, lb_s, tail_s, rs_s, carry_s, acc_s = refs[5 * hp:]
    qi = pl.program_id(2)
    tq, d = q_refs[0].shape
    scale = d ** -0.5
    qs = [r[...].astype(BF16) for r in q_refs]
    triu = _triu(tq)

    def rows_of(c):
        return pl.ds(pl.multiple_of(c * tq, tq), tq)

    def score_stage(s):
        for h in range(hp):
            z_s[lax.rem(s, 2), h] = _nt_dot(qs[h], k_refs[h][rows_of(qi - s), :].astype(BF16)) * scale

    def weight_stage(s, visible):
        slot = lax.rem(s, 2)
        for h in range(hp):
            log_beta, log_1mb = _log_sigmoid_pair(z_s[slot, h])
            if visible is not None:
                log_1mb = jnp.where(visible, log_1mb, 0.0)
                log_beta = jnp.where(visible, log_beta, NEG_BIG)
            hi, lo = _split_bf16(log_1mb, 2)
            lb_s[slot, h] = log_beta
            tail_s[slot, h] = _dot(hi, triu) + _dot(lo, triu)
            rs_s[slot, h] = jnp.sum(log_1mb, axis=-1, keepdims=True)

    def accumulate_stage(s):
        slot = lax.rem(s, 2)
        for h in range(hp):
            carry = carry_s[h]
            a = jnp.exp(lb_s[slot, h] + tail_s[slot, h] + carry).astype(BF16)
            acc_s[h] += _dot(a, v_refs[h][rows_of(qi - s), :].astype(BF16))
            carry_s[h] = carry + rs_s[slot, h]

    carry_s[...] = jnp.zeros(carry_s.shape, F32)
    acc_s[...] = jnp.zeros(acc_s.shape, F32)
    score_stage(0)

    @pl.when(qi >= 1)
    def _():
        score_stage(1)

    weight_stage(0, _iota((1, tq), 1) < _iota((tq, 1), 0))

    def body(s, _):
        accumulate_stage(s - 2)
        weight_stage(s - 1, None)
        score_stage(s)
        return 0

    lax.fori_loop(2, qi + 1, body, 0)

    @pl.when(qi >= 1)
    def _():
        weight_stage(qi, None)
        accumulate_stage(qi - 1)

    accumulate_stage(qi)
    g = g_ref[layer:layer + 1, :]
    meta_triu = _triu(km_refs[0].shape[0])
    for h in range(hp):
        _, acc = _sb_block(qs[h], km_refs[h][...].astype(BF16), vm_refs[h][...].astype(BF16), scale,
                           carry_s[h], acc_s[h], None, meta_triu)
        o_ref[:, h * LANES:(h + 1) * LANES] = _head_norm(acc, g).astype(o_ref.dtype)


def _sb_prompt(z, gain, layer, lay):
    B, SEQ, H, NM, RP = lay["B"], lay["SEQ"], lay["SB_HEADS"], lay["N_META"], lay["RP"]
    tq = _divisor_tile(SEQ, 256, 16)
    nq = SEQ // tq
    hp = 2 if H % 2 == 0 else 1
    qc, kc, vc = (lay["SQ_OFF"] // LANES, lay["SK_OFF"] // LANES, lay["SV_OFF"] // LANES)
    nl = gain.shape[0]
    mrow = RP // NM
    vmem = 8 * hp * SEQ * LANES * 4 + 24 * 1024 * 1024
    in_specs = ([pl.BlockSpec((nl, LANES), lambda b, g, i: (0, 0))]
                + _head_specs((tq, LANES), hp, lambda b, h, i: (b * nq + i, qc + h))
                + _head_specs((SEQ, LANES), hp, lambda b, h, i: (b, kc + h))
                + _head_specs((SEQ, LANES), hp, lambda b, h, i: (b, vc + h))
                + _head_specs((NM, LANES), hp, lambda b, h, i: (mrow + b, kc + h))
                + _head_specs((NM, LANES), hp, lambda b, h, i: (mrow + b, vc + h)))
    return pl.pallas_call(
        functools.partial(_sb_prompt_kernel, layer=layer, hp=hp),
        grid=(B, H // hp, nq),
        in_specs=in_specs,
        out_specs=pl.BlockSpec((tq, hp * LANES), lambda b, g, i: (b * nq + i, g)),
        out_shape=jax.ShapeDtypeStruct((RP, H * LANES), BF16),
        scratch_shapes=[pltpu.VMEM((2, hp, tq, tq), F32), pltpu.VMEM((2, hp, tq, tq), F32),
                        pltpu.VMEM((2, hp, tq, tq), F32), pltpu.VMEM((2, hp, tq, 1), F32),
                        pltpu.VMEM((hp, tq, 1), F32), pltpu.VMEM((hp, tq, LANES), F32)],
        compiler_params=_params(("parallel", "parallel", "parallel"), vmem),
        name="sb_prompt",
    )(gain, *([z] * (5 * hp)))


def _sb_small_kernel(*refs, layer, heads, pieces, valid_rows, n_steps, pps):
    if n_steps:
        refs = refs[1:]
    g_ref = refs[0]
    q_ref, k_ref, v_ref = (refs[1 + i * pieces:1 + (i + 1) * pieces] for i in range(3))
    rest = refs[1 + 3 * pieces:]
    if n_steps:
        kc_refs, vc_refs = rest[:pps], rest[pps:2 * pps]
        o_ref, carry_ref, acc_ref = rest[2 * pps:]
    else:
        o_ref, carry_ref, acc_ref = rest
    T = o_ref.shape[0]
    d = LANES
    scale = d ** -0.5
    step = pl.program_id(1)
    per_piece = heads // pieces

    def head(piece_refs, h):
        c = (h % per_piece) * d
        return piece_refs[h // per_piece][:, c:c + d].astype(BF16)

    @pl.when(step == 0)
    def _():
        qidx = _iota((T, 1), 0)
        kidx = _iota((1, T), 1)
        visible = (kidx < qidx) & (kidx < valid_rows)
        triu = _triu(T)
        for h in range(heads):
            carry, acc = _sb_block(head(q_ref, h), head(k_ref, h), head(v_ref, h), scale,
                                   jnp.zeros((T, 1), F32), jnp.zeros((T, d), F32), visible, triu)
            carry_ref[h * T:(h + 1) * T, :] = carry
            acc_ref[h * T:(h + 1) * T, :] = acc

    if n_steps:
        @pl.when(step > 0)
        def _():
            page = kc_refs[0].shape[2] // heads

            def rows_of(page_refs, h):
                return jnp.concatenate([r[0, 0, pl.ds(h, page, stride=heads), :] for r in page_refs],
                                       axis=0).astype(BF16)

            z = jnp.concatenate([_nt_dot(head(q_ref, h), rows_of(kc_refs, h)) for h in range(heads)],
                                axis=0) * scale
            log_beta, log_1mb = _log_sigmoid_pair(z)
            triu = _triu(pps * page)
            tail = None
            for part in _split_bf16(log_1mb, 2):
                t = _dot(part, triu)
                tail = t if tail is None else tail + t
            a = jnp.exp(log_beta + tail + carry_ref[...]).astype(BF16)
            carry_ref[...] += jnp.sum(log_1mb, axis=-1, keepdims=True)
            for h in range(heads):
                rows = slice(h * T, (h + 1) * T)
                acc_ref[rows, :] += _dot(a[rows], rows_of(vc_refs, h))

    @pl.when(step == n_steps)
    def _():
        o = _head_norm(acc_ref[...], g_ref[layer:layer + 1, :])
        for h in range(heads):
            o_ref[:, h * d:(h + 1) * d] = o[h * T:(h + 1) * T]


def _sb_small(z, gain, layer, lay, *, row_block, rows, n_seq, valid_rows, cache=None):
    H = lay["SB_HEADS"]
    n_pages = lay["N_PAGES"] if cache is not None else 0
    nl = gain.shape[0]
    width = H * LANES
    cw = math.gcd(lay["SQ_OFF"], width)
    pieces = width // cw

    def ix(f):
        return (lambda b, p, pt: f(b, p, pt)) if n_pages else (lambda b, p: f(b, p, None))

    in_specs = [pl.BlockSpec((nl, LANES), ix(lambda b, p, pt: (0, 0)))]
    in_specs += [pl.BlockSpec((rows, cw), ix((lambda c: lambda b, p, pt: (row_block + b, c))(off // cw + i)))
                 for off in (lay["SQ_OFF"], lay["SK_OFF"], lay["SV_OFF"]) for i in range(pieces)]
    args = [gain] + [z] * (3 * pieces)
    pps = _pages_per_step(n_pages, 2)
    n_steps = n_pages // pps
    if n_pages:
        kc, vc, page_table = cache

        def page(i):
            return lambda b, p, pt: (layer, pt[b, n_pages - jnp.maximum(p, 1) * pps + i], 0, 0)

        in_specs += [pl.BlockSpec((1, 1) + kc.shape[2:], page(i)) for i in range(pps)]
        in_specs += [pl.BlockSpec((1, 1) + vc.shape[2:], page(i)) for i in range(pps)]
        args += [kc] * pps + [vc] * pps
    kernel = functools.partial(_sb_small_kernel, layer=layer, heads=H, pieces=pieces,
                               valid_rows=valid_rows, n_steps=n_steps, pps=pps)
    out_spec = pl.BlockSpec((rows, width), ix(lambda b, p, pt: (b, 0)))
    scratch = [pltpu.VMEM((H * rows, 1), F32), pltpu.VMEM((H * rows, LANES), F32)]
    out_shape = jax.ShapeDtypeStruct((n_seq * rows, width), F32)
    params = _params(("parallel", "arbitrary"), 32 * 1024 * 1024)
    if n_pages:
        return pl.pallas_call(
            kernel,
            grid_spec=pltpu.PrefetchScalarGridSpec(
                num_scalar_prefetch=1, grid=(n_seq, n_steps + 1), in_specs=in_specs,
                out_specs=out_spec, scratch_shapes=scratch),
            out_shape=out_shape, compiler_params=params, name="sb_sample",
        )(page_table, *args)
    return pl.pallas_call(
        kernel, grid=(n_seq, 1), in_specs=in_specs, out_specs=out_spec, scratch_shapes=scratch,
        out_shape=out_shape, compiler_params=params, name="sb_meta",
    )(*args)


def _rope_tables(pos, d_map, rope_dim):
    half = rope_dim // 2
    inv = ROPE_THETA ** (-jnp.arange(half, dtype=F32) / half)
    ang = pos.astype(F32)[:, None] * inv[None, :]
    cos, sin = jnp.cos(ang), jnp.sin(ang)
    rows = pos.shape[0]
    pad = jnp.zeros((rows, d_map - rope_dim), F32)
    zero = jnp.zeros((rows, half), F32)
    c = jnp.concatenate([cos, cos, pad + 1.0], axis=1)
    s1 = jnp.concatenate([-sin, zero, pad], axis=1)
    s2 = jnp.concatenate([zero, sin, pad], axis=1)
    rep = LANES // d_map
    return tuple(jnp.tile(t, (1, rep)) for t in (c, s1, s2))


def kernel(x_prompt, x_sample, cache_diff_k, cache_diff_v, cache_sb_k, cache_sb_v, state_hgrn,
           page_table, meta_tokens, norm_mix, norm_ffn, norm_final, w_in, hg_lower_bounds, hg_norm,
           diff_lambda, diff_subln, sb_norm, w_out, w_gate_up, w_down):
    B, SEQ, D = x_prompt.shape
    DB, DEC_SEQ, _ = x_sample.shape
    DEPTH = w_in.shape[0]
    NM = meta_tokens.shape[0]
    HGH = state_hgrn.shape[2]
    DH = cache_diff_v.shape[3]
    DQK = cache_diff_k.shape[4]
    SH = cache_sb_k.shape[3]
    PAGE = cache_sb_k.shape[2]
    NP = page_table.shape[1]
    assert DEC_SEQ <= SAMPLE_ROWS and NM % 16 == 0 and SEQ % NM == 0 and 2 * DQK == LANES
    assert state_hgrn.shape[3:] == (LANES, LANES) and HGH == DH
    HGW, DW, SW = HGH * LANES, DH * LANES, SH * LANES
    RP, RM, RS = B * SEQ, B * NM, DB * SAMPLE_ROWS
    lay = dict(B=B, SEQ=SEQ, DB=DB, DEC_SEQ=DEC_SEQ, N_META=NM, HG_HEADS=HGH, DIFF_HEADS=DH, SB_HEADS=SH,
               N_PAGES=NP, RP=RP, RM=RM, RS=RS,
               DV_OFF=4 * HGW + 2 * DW, SQ_OFF=4 * HGW + 3 * DW, SK_OFF=4 * HGW + 3 * DW + SW,
               SV_OFF=4 * HGW + 3 * DW + 2 * SW)

    sample = jnp.pad(x_sample, ((0, 0), (0, SAMPLE_ROWS - DEC_SEQ), (0, 0)))
    x = jnp.concatenate([x_prompt.reshape(RP, D),
                         jnp.broadcast_to(meta_tokens[None], (B, NM, D)).reshape(RM, D),
                         sample.reshape(RS, D)], axis=0)
    pos = jnp.concatenate([jnp.tile(NM + jnp.arange(SEQ, dtype=jnp.int32), B),
                           jnp.tile(jnp.arange(NM, dtype=jnp.int32), B),
                           jnp.tile(NP * PAGE + jnp.arange(SAMPLE_ROWS, dtype=jnp.int32), DB)])
    tables = _rope_tables(pos, DQK, DQK // 4)

    ck_d = jnp.transpose(cache_diff_k, (0, 1, 3, 4, 2))
    cv_d = cache_diff_v.reshape(cache_diff_v.shape[:2] + (PAGE * DH, LANES))
    ck_s = cache_sb_k.reshape(cache_sb_k.shape[:2] + (PAGE * SH, LANES))
    cv_s = cache_sb_v.reshape(cache_sb_v.shape[:2] + (PAGE * SH, LANES))

    w_in_b, w_out_b, w_gu_b, w_dn_b = w_in, w_out, w_gate_up, w_down
    g_mix = norm_mix.reshape(DEPTH, 1, D)
    g_ffn = norm_ffn.reshape(DEPTH, 1, D)
    g_fin = norm_final.reshape(1, 1, D)
    meta_blk = RP // NM
    samp_blk = (RP + RM) // SAMPLE_ROWS
    small_meta = dict(row_block=meta_blk, rows=NM, n_seq=B, valid_rows=NM)
    small_samp = dict(row_block=samp_blk, rows=SAMPLE_ROWS, n_seq=DB, valid_rows=DEC_SEQ)

    outs = {k: [] for k in ("pk_d", "pv_d", "pk_s", "pv_s", "ps_h", "sk_d", "sv_d", "sk_s", "sv_s", "ss_h")}

    def cache_rows(a, off, width, heads):
        a = a[:, off:off + width]
        p = jnp.concatenate([a[RP:RP + RM].reshape(B, NM, heads, -1), a[:RP].reshape(B, SEQ, heads, -1)], axis=1)
        s = a[RP + RM:].reshape(DB, SAMPLE_ROWS, heads, -1)[:, :DEC_SEQ]
        return p, s

    for l in range(DEPTH):
        hn = _rmsnorm(x, g_mix, l, BF16)
        z = _matmul(hn, w_in_b, l)
        dqk = _rope(z, 2, 2 * DW, tables)

        hg_m, hg_p, s_p = _hgrn_prompt(z, hg_lower_bounds, hg_norm, l, lay)
        hg_s, s_s = _hgrn_sample(z, state_hgrn, hg_lower_bounds, hg_norm, l, lay)
        df_p = _diff_prompt(dqk, z, diff_lambda, diff_subln, l, lay)
        df_m = _diff_small(dqk, z, diff_lambda, diff_subln, l, lay, **small_meta)
        df_s = _diff_small(dqk, z, diff_lambda, diff_subln, l, lay, **small_samp,
                           cache=(ck_d, cv_d, page_table))
        sb_p = _sb_prompt(z, sb_norm, l, lay)
        sb_m = _sb_small(z, sb_norm, l, lay, **small_meta)
        sb_s = _sb_small(z, sb_norm, l, lay, **small_samp, cache=(ck_s, cv_s, page_table))

        mix = jnp.concatenate([
            jnp.concatenate([hg_p, df_p, sb_p], axis=1),
            jnp.concatenate([hg_m, df_m.astype(BF16), sb_m.astype(BF16)], axis=1),
            jnp.concatenate([hg_s.astype(BF16), df_s.astype(BF16), sb_s.astype(BF16)], axis=1)], axis=0)
        x = _matmul(mix, w_out_b, l, res=x)
        h = _ffn_up(_rmsnorm(x, g_ffn, l, BF16), w_gu_b, l)
        x = _matmul(h, w_dn_b, l, res=x)

        for name_p, name_s, (p, s) in (
                ("pk_d", "sk_d", cache_rows(dqk, DW, DW, 2 * DH)),
                ("pv_d", "sv_d", cache_rows(z, lay["DV_OFF"], DW, DH)),
                ("pk_s", "sk_s", cache_rows(z, lay["SK_OFF"], SW, SH)),
                ("pv_s", "sv_s", cache_rows(z, lay["SV_OFF"], SW, SH))):
            outs[name_p].append(p)
            outs[name_s].append(s)
        outs["ps_h"].append(s_p)
        outs["ss_h"].append(s_s)

    y = _rmsnorm(x, g_fin, 0, F32)
    y_prompt = y[:RP].reshape(B, SEQ, D)
    y_sample = y[RP + RM:].reshape(DB, SAMPLE_ROWS, D)[:, :DEC_SEQ]
    st = lambda k: jnp.stack(outs[k], axis=0)
    return (y_prompt, y_sample, st("pk_d"), st("pv_d"), st("pk_s"), st("pv_s"), st("ps_h"),
            st("sk_d"), st("sv_d"), st("sk_s"), st("sv_s"), st("ss_h"))
```

```python
import functools
import math

import jax
import jax.numpy as jnp
from jax import lax
from jax.experimental import pallas as pl
from jax.experimental.pallas import tpu as pltpu

F32 = jnp.float32
BF16 = jnp.bfloat16
EPS = 1e-6
NEG_BIG = -1e30
ROPE_THETA = 500000.0
LANES = 128
SAMPLE_ROWS = 8
V7X_VMEM_BYTES = 64 * 1024 * 1024
VMEM_CAP = V7X_VMEM_BYTES - 8 * 1024 * 1024


def _params(semantics, vmem_bytes):
    limit = int(min(max(vmem_bytes, 16 * 1024 * 1024), VMEM_CAP))
    return pltpu.CompilerParams(dimension_semantics=semantics, vmem_limit_bytes=limit)


def _divisor_tile(n, target, multiple):
    best = None
    for t in range(multiple, min(n, target) + 1, multiple):
        if n % t == 0:
            best = t
    return best if best is not None else n


def _pages_per_step(n_pages, most):
    for pps in (4, 2):
        if pps <= most and n_pages > 0 and n_pages % pps == 0:
            return pps
    return 1


def _nt_dot(a, b):
    return lax.dot_general(a, b, (((1,), (1,)), ((), ())), preferred_element_type=F32)


def _dot(a, b):
    return jnp.dot(a, b, preferred_element_type=F32)


def _iota(shape, dim):
    return lax.broadcasted_iota(jnp.int32, shape, dim)


def _split_bf16(x, parts):
    out = []
    r = x
    for i in range(parts):
        p = r.astype(BF16)
        out.append(p)
        if i + 1 < parts:
            r = r - p.astype(F32)
    return out


def _sigmoid_pair(z):
    e = jnp.exp(-jnp.abs(z))
    r = 1.0 / (1.0 + e)
    er = e * r
    pos = z >= 0
    return jnp.where(pos, r, er), jnp.where(pos, er, r)


def _silu(z):
    return z * _sigmoid_pair(z)[0]


def _head_norm(o, g):
    return o * lax.rsqrt(jnp.mean(o * o, axis=-1, keepdims=True) + EPS) * g


def _rmsnorm_kernel(x_ref, g_ref, o_ref):
    x = x_ref[...]
    y = x * lax.rsqrt(jnp.mean(x * x, axis=-1, keepdims=True) + EPS)
    o_ref[...] = (y * g_ref[...]).astype(o_ref.dtype)


def _rmsnorm(x, gains, layer, out_dtype):
    M, D = x.shape
    tr = _divisor_tile(M, 256, 16)
    vmem = 2 * tr * D * (4 + jnp.dtype(out_dtype).itemsize) + 3 * tr * D * 4
    return pl.pallas_call(
        _rmsnorm_kernel,
        grid=(M // tr,),
        in_specs=[pl.BlockSpec((tr, D), lambda i: (i, 0)),
                  pl.BlockSpec((None, 1, D), lambda i: (layer, 0, 0))],
        out_specs=pl.BlockSpec((tr, D), lambda i: (i, 0)),
        out_shape=jax.ShapeDtypeStruct((M, D), out_dtype),
        compiler_params=_params(("parallel",), vmem),
        name="rmsnorm",
    )(x, gains)


def _rope_kernel(x_ref, c_ref, s1_ref, s2_ref, o_ref):
    c = c_ref[...]
    s1 = s1_ref[...]
    s2 = s2_ref[...]
    for j in range(x_ref.shape[1] // LANES):
        x = x_ref[:, j * LANES:(j + 1) * LANES]
        up = pltpu.roll(x, LANES - 8, 1)
        dn = pltpu.roll(x, 8, 1)
        o_ref[:, j * LANES:(j + 1) * LANES] = x * c + up * s1 + dn * s2


def _rope(z, col_block, width, tables):
    M = z.shape[0]
    tr = _divisor_tile(M, 512, 8)
    tab = pl.BlockSpec((tr, LANES), lambda i: (i, 0))
    vmem = 4 * tr * width * 4 + 6 * tr * LANES * 4 + 4 * tr * LANES * 4
    return pl.pallas_call(
        _rope_kernel,
        grid=(M // tr,),
        in_specs=[pl.BlockSpec((tr, width), lambda i: (i, col_block)), tab, tab, tab],
        out_specs=pl.BlockSpec((tr, width), lambda i: (i, 0)),
        out_shape=jax.ShapeDtypeStruct((M, width), F32),
        compiler_params=_params(("parallel",), vmem),
        name="rope",
    )(z, *tables)


def _matmul_kernel(a_ref, w_ref, *rest, nk, has_res):
    o_ref = rest[-1]
    p = _dot(a_ref[...], w_ref[...].astype(BF16))
    if has_res:
        first = p + rest[0][...]
    else:
        first = p
    if nk == 1:
        o_ref[...] = first
    else:
        k = pl.program_id(2)

        @pl.when(k == 0)
        def _():
            o_ref[...] = first

        @pl.when(k > 0)
        def _():
            o_ref[...] += p


def _matmul_tiles(M, K, N):
    wide = K > 4096
    bm = _divisor_tile(M, 832 if wide else 1664, 16)
    bn = _divisor_tile(N, 256 if wide else 512, LANES)
    return bm, K, bn


def _matmul(a, w, layer, res=None):
    M, K = a.shape
    N = w.shape[2]
    bm, bk, bn = _matmul_tiles(M, K, N)
    nk = K // bk
    a_bufs = 1 if nk == 1 else 2
    a_kw = dict(pipeline_mode=pl.Buffered(1)) if nk == 1 else {}
    in_specs = [pl.BlockSpec((bm, bk), lambda i, j, k: (i, k), **a_kw),
                pl.BlockSpec((None, bk, bn), lambda i, j, k: (layer, k, j))]
    args = [a, w]
    if res is not None:
        in_specs.append(pl.BlockSpec((bm, bn), lambda i, j, k: (i, j)))
        args.append(res)
    vmem = (a_bufs * bm * bk * 2 + 2 * bk * bn * w.dtype.itemsize + bk * bn * 2
            + (4 if res is not None else 2) * bm * bn * 4 + 2 * bm * bn * 4)
    return pl.pallas_call(
        functools.partial(_matmul_kernel, nk=nk, has_res=res is not None),
        grid=(M // bm, N // bn, nk),
        in_specs=in_specs,
        out_specs=pl.BlockSpec((bm, bn), lambda i, j, k: (i, j)),
        out_shape=jax.ShapeDtypeStruct((M, N), F32),
        compiler_params=_params(("parallel", "parallel", "arbitrary"), vmem),
        name="matmul",
    )(*args)


def _ffn_up_kernel(a_ref, wg_ref, wu_ref, o_ref):
    a = a_ref[...]
    g = _dot(a, wg_ref[...].astype(BF16))
    u = _dot(a, wu_ref[...].astype(BF16))
    o_ref[...] = (_silu(g) * u).astype(o_ref.dtype)


def _ffn_up(a, w_gu, layer):
    M, K = a.shape
    F = w_gu.shape[2] // 2
    bm = _divisor_tile(M, 1664, 16)
    bn = _divisor_tile(F, 256, LANES)
    nf = F // bn
    vmem = (bm * K * 2 + 4 * K * bn * w_gu.dtype.itemsize + 2 * K * bn * 2 + 2 * bm * bn * 2
            + 4 * bm * bn * 4)
    return pl.pallas_call(
        _ffn_up_kernel,
        grid=(M // bm, nf),
        in_specs=[pl.BlockSpec((bm, K), lambda i, j: (i, 0), pipeline_mode=pl.Buffered(1)),
                  pl.BlockSpec((None, K, bn), lambda i, j: (layer, 0, j)),
                  pl.BlockSpec((None, K, bn), lambda i, j: (layer, 0, nf + j))],
        out_specs=pl.BlockSpec((bm, bn), lambda i, j: (i, j)),
        out_shape=jax.ShapeDtypeStruct((M, F), BF16),
        compiler_params=_params(("parallel", "parallel"), vmem),
        name="ffn_up",
    )(a, w_gu, w_gu)


def _lower_bound(lbraw_ref, layer):
    raw = lbraw_ref[...]
    e = jnp.exp(raw - jnp.max(raw, axis=0, keepdims=True))
    total = jnp.sum(e, axis=0, keepdims=True)
    if layer == 0:
        return jnp.zeros_like(total)
    return jnp.sum(e[1:layer + 1], axis=0, keepdims=True) / total


def _gla_chunk(st, hq, hf, hi, lb, *, valid_rows, sub):
    C, K = hq.shape
    sig, sig_neg = _sigmoid_pair(hf)
    q = _silu(hq) * (K ** -0.5)
    gl = jnp.log(lb + (1.0 - lb) * sig)
    k = (1.0 - lb) * sig_neg
    v = hi
    if valid_rows < C:
        ok = _iota((C, 1), 0) < valid_rows
        gl = jnp.where(ok, gl, 0.0)
        k = jnp.where(ok, k, 0.0)

    row = _iota((C, C), 0)
    col = _iota((C, C), 1)
    tril = (col <= row).astype(BF16)
    G = None
    for part in _split_bf16(gl, 3):
        t = _dot(tril, part)
        G = t if G is None else G + t

    o = _nt_dot((q * jnp.exp(G)).astype(BF16), st.astype(BF16))

    A = jnp.zeros((C, C), F32)
    b = sub
    while b < C:
        nb = C // b
        ends = [G[r * b + b - 1:r * b + b] for r in range(nb)]
        e_cur = jnp.concatenate([jnp.broadcast_to(e, (b, K)) for e in ends], axis=0)
        e_prev = jnp.concatenate([jnp.zeros((b, K), F32)]
                                 + [jnp.broadcast_to(e, (b, K)) for e in ends[:-1]], axis=0)
        qb = (q * jnp.exp(G - e_prev)).astype(BF16)
        kb = (k * jnp.exp(e_cur - G)).astype(BF16)
        sh = int(math.log2(b))
        in_level = (((row >> (sh + 1)) == (col >> (sh + 1)))
                    & (((row >> sh) & 1) == 1) & (((col >> sh) & 1) == 0))
        A = jnp.where(in_level, _nt_dot(qb, kb), A)
        b *= 2

    lane = _iota((sub, C), 1)
    trow = _iota((sub, 1), 0)
    blocks = []
    for i in range(C // sub):
        Gi = G[i * sub:(i + 1) * sub]
        qi = q[i * sub:(i + 1) * sub]
        Ai = A[i * sub:(i + 1) * sub]
        for s in range(sub):
            r = i * sub + s
            expo = jnp.where(trow >= s, Gi - G[r:r + 1], NEG_BIG)
            colv = jnp.sum(qi * k[r:r + 1] * jnp.exp(expo), axis=-1, keepdims=True)
            Ai = jnp.where(lane == r, colv, Ai)
        blocks.append(Ai)
    A = jnp.concatenate(blocks, axis=0) if len(blocks) > 1 else blocks[0]
    o = o + _dot(A.astype(BF16), v.astype(BF16))

    g_last = G[C - 1:C]
    k_dec = (k * jnp.exp(g_last - G)).astype(BF16)
    st_new = st * jnp.exp(g_last) + lax.dot_general(
        v.astype(BF16), k_dec, (((0,), (0,)), ((), ())), preferred_element_type=F32)
    return st_new, o


def _hgrn_prompt_kernel(lbraw_ref, gn_ref, mq_ref, mf_ref, mi_ref, mg_ref,
                        hq_ref, hf_ref, hi_ref, hg_ref, om_ref, op_ref, s_ref, *, layer, chunk, hp):
    gn = gn_ref[layer:layer + 1, :]
    n_meta = mq_ref.shape[0]
    K = LANES
    lanes = [slice(h * K, (h + 1) * K) for h in range(hp)]
    lb_all = _lower_bound(lbraw_ref, layer)
    lbs = [lb_all[:, c] for c in lanes]

    states = []
    for h, c in enumerate(lanes):
        st, o = _gla_chunk(jnp.zeros((K, K), F32), mq_ref[:, c], mf_ref[:, c], mi_ref[:, c], lbs[h],
                           valid_rows=n_meta, sub=min(16, n_meta))
        om_ref[:, c] = (_head_norm(o, gn) * _silu(mg_ref[:, c])).astype(om_ref.dtype)
        states.append(st)

    def body(ci, states):
        rows = pl.ds(pl.multiple_of(ci * chunk, chunk), chunk)
        out = []
        for h, c in enumerate(lanes):
            st, o = _gla_chunk(states[h], hq_ref[rows, c], hf_ref[rows, c], hi_ref[rows, c], lbs[h],
                               valid_rows=chunk, sub=16)
            op_ref[rows, c] = (_head_norm(o, gn) * _silu(hg_ref[rows, c])).astype(op_ref.dtype)
            out.append(st)
        return tuple(out)

    states = lax.fori_loop(0, hq_ref.shape[0] // chunk, body, tuple(states))
    for h in range(hp):
        s_ref[h] = states[h].T


def _hgrn_prompt(z, lb_raw, gnorm, layer, lay):
    B, SEQ, H, NM, RP, RM = lay["B"], lay["SEQ"], lay["HG_HEADS"], lay["N_META"], lay["RP"], lay["RM"]
    chunk = _divisor_tile(SEQ, 128, 16)
    nl = lb_raw.shape[0]
    hp = 2 if H % 2 == 0 else 1
    G, W = H // hp, hp * LANES

    def col(k):
        return lambda b, g: (b, k * G + g)

    def mcol(k):
        return lambda b, g: (RP // NM + b, k * G + g)

    prompt = [pl.BlockSpec((SEQ, W), col(k)) for k in range(4)]
    meta = [pl.BlockSpec((NM, W), mcol(k)) for k in range(4)]
    vmem = 16 * SEQ * W * 4 + 8 * 1024 * 1024
    return pl.pallas_call(
        functools.partial(_hgrn_prompt_kernel, layer=layer, chunk=chunk, hp=hp),
        grid=(B, G),
        in_specs=[pl.BlockSpec((nl, W), lambda b, g: (0, g)),
                  pl.BlockSpec((nl, LANES), lambda b, g: (0, 0))] + meta + prompt,
        out_specs=[pl.BlockSpec((NM, W), lambda b, g: (b, g)),
                   pl.BlockSpec((SEQ, W), lambda b, g: (b, g)),
                   pl.BlockSpec((None, hp, LANES, LANES), lambda b, g: (b, g, 0, 0))],
        out_shape=[jax.ShapeDtypeStruct((RM, H * LANES), BF16),
                   jax.ShapeDtypeStruct((RP, H * LANES), BF16),
                   jax.ShapeDtypeStruct((B, H, LANES, LANES), F32)],
        compiler_params=_params(("parallel", "parallel"), vmem),
        name="hgrn_prompt",
    )(lb_raw, gnorm, z, z, z, z, z, z, z, z)


def _hgrn_sample_kernel(lbraw_ref, gn_ref, s0_ref, hq_ref, hf_ref, hi_ref, hg_ref, o_ref, s_ref,
                        *, layer, valid_rows):
    lb = _lower_bound(lbraw_ref, layer)
    gn = gn_ref[layer:layer + 1, :]
    st, o = _gla_chunk(s0_ref[...].T, hq_ref[...], hf_ref[...], hi_ref[...], lb,
                       valid_rows=valid_rows, sub=hq_ref.shape[0])
    o_ref[...] = _head_norm(o, gn) * _silu(hg_ref[...])
    s_ref[...] = st.T


def _hgrn_sample(z, state, lb_raw, gnorm, layer, lay):
    DB, H, RS = lay["DB"], lay["HG_HEADS"], lay["RS"]
    base = (lay["RP"] + lay["RM"]) // SAMPLE_ROWS
    nl = lb_raw.shape[0]
    rows = [pl.BlockSpec((SAMPLE_ROWS, LANES), (lambda k: lambda b, h: (base + b, k * H + h))(k))
            for k in range(4)]
    return pl.pallas_call(
        functools.partial(_hgrn_sample_kernel, layer=layer, valid_rows=lay["DEC_SEQ"]),
        grid=(DB, H),
        in_specs=[pl.BlockSpec((nl, LANES), lambda b, h: (0, h)),
                  pl.BlockSpec((nl, LANES), lambda b, h: (0, 0)),
                  pl.BlockSpec((None, None, None, LANES, LANES), lambda b, h: (layer, b, h, 0, 0))] + rows,
        out_specs=[pl.BlockSpec((SAMPLE_ROWS, LANES), lambda b, h: (b, h)),
                   pl.BlockSpec((None, None, LANES, LANES), lambda b, h: (b, h, 0, 0))],
        out_shape=[jax.ShapeDtypeStruct((RS, H * LANES), F32),
                   jax.ShapeDtypeStruct((DB, H, LANES, LANES), F32)],
        compiler_params=_params(("parallel", "parallel"), 16 * 1024 * 1024),
        name="hgrn_sample",
    )(lb_raw, gnorm, state, z, z, z, z)


def _diff_lambda(lam_ref, layer):
    lp = lam_ref[...]
    a = jnp.sum(lp[0:1] * lp[1:2], axis=-1, keepdims=True)
    b = jnp.sum(lp[2:3] * lp[3:4], axis=-1, keepdims=True)
    lam_init = 0.8 - 0.6 * math.exp(-0.3 * layer)
    return jnp.exp(a) - jnp.exp(b) + lam_init, lam_init


def _softmax_step(carry, s, v):
    mx, l, acc = carry
    m_new = jnp.maximum(mx, jnp.max(s, axis=-1, keepdims=True))
    alpha = jnp.exp(mx - m_new)
    p = jnp.exp(s - m_new)
    return (m_new, alpha * l + jnp.sum(p, axis=-1, keepdims=True),
            alpha * acc + _dot(p.astype(BF16), v))


def _diff_prompt_kernel(lam_ref, g_ref, *refs, layer, hp):
    q_refs, k_refs, v_refs, km_refs, vm_refs = (refs[i * hp:(i + 1) * hp] for i in range(5))
    o_ref = refs[5 * hp]
    qi = pl.program_id(2)
    tq, width = q_refs[0].shape
    d = width // 2
    scale = d ** -0.5
    lam, lam_init = _diff_lambda(lam_ref, layer)
    g = g_ref[layer:layer + 1, :]
    qs = [[r[:, m * d:(m + 1) * d].astype(BF16) for m in range(2)] for r in q_refs]

    def scores(h, k):
        return jnp.concatenate(
            [_nt_dot(qs[h][m], k[:, m * d:(m + 1) * d].astype(BF16)) for m in range(2)], axis=0) * scale

    state = []
    for h in range(hp):
        s = scores(h, km_refs[h][...])
        mx = jnp.max(s, axis=-1, keepdims=True)
        p = jnp.exp(s - mx)
        state.append((mx, jnp.sum(p, axis=-1, keepdims=True),
                      _dot(p.astype(BF16), vm_refs[h][...].astype(BF16))))

    diag = pl.ds(pl.multiple_of(qi * tq, tq), tq)
    causal = _iota((1, tq), 1) <= _iota((tq, 1), 0)
    causal = jnp.concatenate([causal, causal], axis=0)
    for h in range(hp):
        s = jnp.where(causal, scores(h, k_refs[h][diag, :]), NEG_BIG)
        state[h] = _softmax_step(state[h], s, v_refs[h][diag, :].astype(BF16))

    def body(j, state):
        rows = pl.ds(pl.multiple_of(j * tq, tq), tq)
        return tuple(_softmax_step(state[h], scores(h, k_refs[h][rows, :]), v_refs[h][rows, :].astype(BF16))
                     for h in range(hp))

    state = lax.fori_loop(0, qi, body, tuple(state))
    for h in range(hp):
        _, l, acc = state[h]
        o = acc / l
        o = o[:tq] - lam * o[tq:]
        o_ref[:, h * LANES:(h + 1) * LANES] = (_head_norm(o, g) * (1.0 - lam_init)).astype(o_ref.dtype)


def _head_specs(shape, hp, index):
    return [pl.BlockSpec(shape, (lambda n: lambda b, g, i: index(b, g * hp + n, i))(n)) for n in range(hp)]


def _diff_prompt(dqk, z, lam, gain, layer, lay):
    B, SEQ, H, NM, RP = lay["B"], lay["SEQ"], lay["DIFF_HEADS"], lay["N_META"], lay["RP"]
    tq = _divisor_tile(SEQ, 256, 16)
    nq = SEQ // tq
    hp = 2 if H % 2 == 0 else 1
    vcol = lay["DV_OFF"] // LANES
    nl = gain.shape[0]
    mrow = RP // NM
    vmem = 8 * hp * SEQ * LANES * 4 + 16 * 1024 * 1024
    in_specs = ([pl.BlockSpec((None,) + lam.shape[1:], lambda b, g, i: (layer, 0, 0)),
                 pl.BlockSpec((nl, LANES), lambda b, g, i: (0, 0))]
                + _head_specs((tq, LANES), hp, lambda b, h, i: (b * nq + i, h))
                + _head_specs((SEQ, LANES), hp, lambda b, h, i: (b, H + h))
                + _head_specs((SEQ, LANES), hp, lambda b, h, i: (b, vcol + h))
                + _head_specs((NM, LANES), hp, lambda b, h, i: (mrow + b, H + h))
                + _head_specs((NM, LANES), hp, lambda b, h, i: (mrow + b, vcol + h)))
    return pl.pallas_call(
        functools.partial(_diff_prompt_kernel, layer=layer, hp=hp),
        grid=(B, H // hp, nq),
        in_specs=in_specs,
        out_specs=pl.BlockSpec((tq, hp * LANES), lambda b, g, i: (b * nq + i, g)),
        out_shape=jax.ShapeDtypeStruct((RP, H * LANES), BF16),
        compiler_params=_params(("parallel", "parallel", "parallel"), vmem),
        name="diff_prompt",
    )(lam, gain, *([dqk] * hp + [dqk] * hp + [z] * hp + [dqk] * hp + [z] * hp))


def _diff_small_kernel(*refs, layer, heads, valid_rows, n_steps, pps):
    if n_steps:
        lam_ref, g_ref, qk_ref, v_ref = refs[1:5]
        kc_refs, vc_refs = refs[5:5 + pps], refs[5 + pps:5 + 2 * pps]
        o_ref, m_ref, l_ref, acc_ref = refs[5 + 2 * pps:]
    else:
        lam_ref, g_ref, qk_ref, v_ref, o_ref, m_ref, l_ref, acc_ref = refs
    T = qk_ref.shape[0]
    d = qk_ref.shape[1] // (4 * heads)
    scale = d ** -0.5
    step = pl.program_id(1)

    def q_map(m):
        return qk_ref[:, m * d:(m + 1) * d].astype(BF16)

    @pl.when(step == 0)
    def _():
        qidx = _iota((T, 1), 0)
        kidx = _iota((1, T), 1)
        visible = (kidx <= qidx) & (kidx < valid_rows)
        for m in range(2 * heads):
            h = m // 2
            km = qk_ref[:, (2 * heads + m) * d:(2 * heads + m + 1) * d].astype(BF16)
            s = jnp.where(visible, _nt_dot(q_map(m), km) * scale, NEG_BIG)
            mx = jnp.max(s, axis=-1, keepdims=True)
            p = jnp.exp(s - mx)
            rows = slice(m * T, (m + 1) * T)
            m_ref[rows, :] = mx
            l_ref[rows, :] = jnp.sum(p, axis=-1, keepdims=True)
            acc_ref[rows, :] = _dot(p.astype(BF16), v_ref[:, h * LANES:(h + 1) * LANES].astype(BF16))

    if n_steps:
        @pl.when(step > 0)
        def _():
            page = kc_refs[0].shape[-1]

            def keys_t(m):
                return jnp.concatenate([r[0, 0, m] for r in kc_refs], axis=1).astype(BF16)

            def values(h):
                return jnp.concatenate([r[0, 0, pl.ds(h, page, stride=heads), :] for r in vc_refs],
                                       axis=0).astype(BF16)

            s = jnp.concatenate([_dot(q_map(m), keys_t(m)) for m in range(2 * heads)],
                                axis=0) * scale
            mx = m_ref[...]
            m_new = jnp.maximum(mx, jnp.max(s, axis=-1, keepdims=True))
            alpha = jnp.exp(mx - m_new)
            p = jnp.exp(s - m_new)
            m_ref[...] = m_new
            l_ref[...] = alpha * l_ref[...] + jnp.sum(p, axis=-1, keepdims=True)
            pb = p.astype(BF16)
            for h in range(heads):
                rows = slice(2 * h * T, (2 * h + 2) * T)
                acc_ref[rows, :] = alpha[rows] * acc_ref[rows, :] + _dot(pb[rows], values(h))

    @pl.when(step == n_steps)
    def _():
        lam, lam_init = _diff_lambda(lam_ref, layer)
        o = acc_ref[...] / l_ref[...]
        g = g_ref[layer:layer + 1, :]
        for h in range(heads):
            oh = o[2 * h * T:(2 * h + 1) * T] - lam * o[(2 * h + 1) * T:(2 * h + 2) * T]
            o_ref[:, h * LANES:(h + 1) * LANES] = _head_norm(oh, g) * (1.0 - lam_init)


def _diff_small(dqk, z, lam, gain, layer, lay, *, row_block, rows, n_seq, valid_rows, cache=None):
    H = lay["DIFF_HEADS"]
    n_pages = lay["N_PAGES"] if cache is not None else 0
    pps = _pages_per_step(n_pages, 4)
    n_steps = n_pages // pps
    nl = gain.shape[0]
    vcol = lay["DV_OFF"] // (H * LANES)

    def ix(f):
        return (lambda b, p, pt: f(b, p, pt)) if n_pages else (lambda b, p: f(b, p, None))

    in_specs = [pl.BlockSpec((None,) + lam.shape[1:], ix(lambda b, p, pt: (layer, 0, 0))),
                pl.BlockSpec((nl, LANES), ix(lambda b, p, pt: (0, 0))),
                pl.BlockSpec((rows, 2 * H * LANES), ix(lambda b, p, pt: (row_block + b, 0))),
                pl.BlockSpec((rows, H * LANES), ix(lambda b, p, pt: (row_block + b, vcol)))]
    args = [lam, gain, dqk, z]
    if n_pages:
        kc, vc, page_table = cache

        def page(i):
            return lambda b, p, pt: (layer, pt[b, (jnp.maximum(p, 1) - 1) * pps + i]) + (0,) * (kc.ndim - 2)

        def vpage(i):
            return lambda b, p, pt: (layer, pt[b, (jnp.maximum(p, 1) - 1) * pps + i], 0, 0)

        in_specs += [pl.BlockSpec((1, 1) + kc.shape[2:], page(i)) for i in range(pps)]
        in_specs += [pl.BlockSpec((1, 1) + vc.shape[2:], vpage(i)) for i in range(pps)]
        args += [kc] * pps + [vc] * pps
    n_rows = 2 * H * rows
    kernel = functools.partial(_diff_small_kernel, layer=layer, heads=H, valid_rows=valid_rows,
                               n_steps=n_steps, pps=pps)
    out_spec = pl.BlockSpec((rows, H * LANES), ix(lambda b, p, pt: (b, 0)))
    scratch = [pltpu.VMEM((n_rows, 1), F32), pltpu.VMEM((n_rows, 1), F32), pltpu.VMEM((n_rows, LANES), F32)]
    out_shape = jax.ShapeDtypeStruct((n_seq * rows, H * LANES), F32)
    params = _params(("parallel", "arbitrary"), 32 * 1024 * 1024)
    if n_pages:
        return pl.pallas_call(
            kernel,
            grid_spec=pltpu.PrefetchScalarGridSpec(
                num_scalar_prefetch=1, grid=(n_seq, n_steps + 1), in_specs=in_specs,
                out_specs=out_spec, scratch_shapes=scratch),
            out_shape=out_shape, compiler_params=params, name="diff_sample",
        )(page_table, *args)
    return pl.pallas_call(
        kernel, grid=(n_seq, 1), in_specs=in_specs, out_specs=out_spec, scratch_shapes=scratch,
        out_shape=out_shape, compiler_params=params, name="diff_meta",
    )(*args)


def _log_sigmoid_pair(z):
    log_beta = jnp.minimum(z, 0.0) - jnp.log(1.0 + jnp.exp(-jnp.abs(z)))
    return log_beta, log_beta - z


def _sb_block(q, k, v, scale, carry, acc, visible, triu):
    z = _nt_dot(q, k) * scale
    log_beta, log_1mb = _log_sigmoid_pair(z)
    if visible is not None:
        log_1mb = jnp.where(visible, log_1mb, 0.0)
    tail = None
    for part in _split_bf16(log_1mb, 2):
        t = _dot(part, triu)
        tail = t if tail is None else tail + t
    w = log_beta + tail + carry
    if visible is not None:
        w = jnp.where(visible, w, NEG_BIG)
    acc = acc + _dot(jnp.exp(w).astype(BF16), v)
    return carry + jnp.sum(log_1mb, axis=-1, keepdims=True), acc


def _triu(n):
    return (_iota((n, n), 0) > _iota((n, n), 1)).astype(BF16)


def _sb_prompt_kernel(g_ref, *refs, layer, hp):
    q_refs, k_refs, v_refs, km_refs, vm_refs = (refs[i * hp:(i + 1) * hp] for i in range(5))
    o_ref, z_s, lb_s, tail_s, rs_s, carry_s, acc_s = refs[5 * hp:]
    qi = pl.program_id(2)
    tq, d = q_refs[0].shape
    scale = d ** -0.5
    qs = [r[...].astype(BF16) for r in q_refs]
    triu = _triu(tq)

    def rows_of(c):
        return pl.ds(pl.multiple_of(c * tq, tq), tq)

    def score_stage(s):
        for h in range(hp):
            z_s[lax.rem(s, 2), h] = _nt_dot(qs[h], k_refs[h][rows_of(qi - s), :].astype(BF16)) * scale

    def weight_stage(s, visible):
        slot = lax.rem(s, 2)
        for h in range(hp):
            log_beta, log_1mb = _log_sigmoid_pair(z_s[slot, h])
            if visible is not None:
                log_1mb = jnp.where(visible, log_1mb, 0.0)
                log_beta = jnp.where(visible, log_beta, NEG_BIG)
            hi, lo = _split_bf16(log_1mb, 2)
            lb_s[slot, h] = log_beta
            tail_s[slot, h] = _dot(hi, triu) + _dot(lo, triu)
            rs_s[slot, h] = jnp.sum(log_1mb, axis=-1, keepdims=True)

    def accumulate_stage(s):
        slot = lax.rem(s, 2)
        for h in range(hp):
            carry = carry_s[h]
            a = jnp.exp(lb_s[slot, h] + tail_s[slot, h] + carry).astype(BF16)
            acc_s[h] += _dot(a, v_refs[h][rows_of(qi - s), :].astype(BF16))
            carry_s[h] = carry + rs_s[slot, h]

    carry_s[...] = jnp.zeros(carry_s.shape, F32)
    acc_s[...] = jnp.zeros(acc_s.shape, F32)
    score_stage(0)

    @pl.when(qi >= 1)
    def _():
        score_stage(1)

    weight_stage(0, _iota((1, tq), 1) < _iota((tq, 1), 0))

    def body(s, _):
        accumulate_stage(s - 2)
        weight_stage(s - 1, None)
        score_stage(s)
        return 0

    lax.fori_loop(2, qi + 1, body, 0)

    @pl.when(qi >= 1)
    def _():
        weight_stage(qi, None)
        accumulate_stage(qi - 1)

    accumulate_stage(qi)
    g = g_ref[layer:layer + 1, :]
    meta_triu = _triu(km_refs[0].shape[0])
    for h in range(hp):
        _, acc = _sb_block(qs[h], km_refs[h][...].astype(BF16), vm_refs[h][...].astype(BF16), scale,
                           carry_s[h], acc_s[h], None, meta_triu)
        o_ref[:, h * LANES:(h + 1) * LANES] = _head_norm(acc, g).astype(o_ref.dtype)


def _sb_prompt(z, gain, layer, lay):
    B, SEQ, H, NM, RP = lay["B"], lay["SEQ"], lay["SB_HEADS"], lay["N_META"], lay["RP"]
    tq = _divisor_tile(SEQ, 256, 16)
    nq = SEQ // tq
    hp = 4 if H % 4 == 0 else (2 if H % 2 == 0 else 1)
    qc, kc, vc = (lay["SQ_OFF"] // LANES, lay["SK_OFF"] // LANES, lay["SV_OFF"] // LANES)
    nl = gain.shape[0]
    mrow = RP // NM
    vmem = 8 * hp * SEQ * LANES * 4 + 24 * 1024 * 1024
    in_specs = ([pl.BlockSpec((nl, LANES), lambda b, g, i: (0, 0))]
                + _head_specs((tq, LANES), hp, lambda b, h, i: (b * nq + i, qc + h))
                + _head_specs((SEQ, LANES), hp, lambda b, h, i: (b, kc + h))
                + _head_specs((SEQ, LANES), hp, lambda b, h, i: (b, vc + h))
                + _head_specs((NM, LANES), hp, lambda b, h, i: (mrow + b, kc + h))
                + _head_specs((NM, LANES), hp, lambda b, h, i: (mrow + b, vc + h)))
    return pl.pallas_call(
        functools.partial(_sb_prompt_kernel, layer=layer, hp=hp),
        grid=(B, H // hp, nq),
        in_specs=in_specs,
        out_specs=pl.BlockSpec((tq, hp * LANES), lambda b, g, i: (b * nq + i, g)),
        out_shape=jax.ShapeDtypeStruct((RP, H * LANES), BF16),
        scratch_shapes=[pltpu.VMEM((2, hp, tq, tq), F32), pltpu.VMEM((2, hp, tq, tq), F32),
                        pltpu.VMEM((2, hp, tq, tq), F32), pltpu.VMEM((2, hp, tq, 1), F32),
                        pltpu.VMEM((hp, tq, 1), F32), pltpu.VMEM((hp, tq, LANES), F32)],
        compiler_params=_params(("parallel", "parallel", "parallel"), vmem),
        name="sb_prompt",
    )(gain, *([z] * (5 * hp)))


def _sb_small_kernel(*refs, layer, heads, pieces, valid_rows, n_steps, pps):
    if n_steps:
        refs = refs[1:]
    g_ref = refs[0]
    q_ref, k_ref, v_ref = (refs[1 + i * pieces:1 + (i + 1) * pieces] for i in range(3))
    rest = refs[1 + 3 * pieces:]
    if n_steps:
        kc_refs, vc_refs = rest[:pps], rest[pps:2 * pps]
        o_ref, carry_ref, acc_ref = rest[2 * pps:]
    else:
        o_ref, carry_ref, acc_ref = rest
    T = o_ref.shape[0]
    d = LANES
    scale = d ** -0.5
    step = pl.program_id(1)
    per_piece = heads // pieces

    def head(piece_refs, h):
        c = (h % per_piece) * d
        return piece_refs[h // per_piece][:, c:c + d].astype(BF16)

    @pl.when(step == 0)
    def _():
        qidx = _iota((T, 1), 0)
        kidx = _iota((1, T), 1)
        visible = (kidx < qidx) & (kidx < valid_rows)
        triu = _triu(T)
        for h in range(heads):
            carry, acc = _sb_block(head(q_ref, h), head(k_ref, h), head(v_ref, h), scale,
                                   jnp.zeros((T, 1), F32), jnp.zeros((T, d), F32), visible, triu)
            carry_ref[h * T:(h + 1) * T, :] = carry
            acc_ref[h * T:(h + 1) * T, :] = acc

    if n_steps:
        @pl.when(step > 0)
        def _():
            page = kc_refs[0].shape[2] // heads

            def rows_of(page_refs, h):
                return jnp.concatenate([r[0, 0, pl.ds(h, page, stride=heads), :] for r in page_refs],
                                       axis=0).astype(BF16)

            z = jnp.concatenate([_nt_dot(head(q_ref, h), rows_of(kc_refs, h)) for h in range(heads)],
                                axis=0) * scale
            log_beta, log_1mb = _log_sigmoid_pair(z)
            triu = _triu(pps * page)
            tail = None
            for part in _split_bf16(log_1mb, 2):
                t = _dot(part, triu)
                tail = t if tail is None else tail + t
            a = jnp.exp(log_beta + tail + carry_ref[...]).astype(BF16)
            carry_ref[...] += jnp.sum(log_1mb, axis=-1, keepdims=True)
            for h in range(heads):
                rows = slice(h * T, (h + 1) * T)
                acc_ref[rows, :] += _dot(a[rows], rows_of(vc_refs, h))

    @pl.when(step == n_steps)
    def _():
        o = _head_norm(acc_ref[...], g_ref[layer:layer + 1, :])
        for h in range(heads):
            o_ref[:, h * d:(h + 1) * d] = o[h * T:(h + 1) * T]


def _sb_small(z, gain, layer, lay, *, row_block, rows, n_seq, valid_rows, cache=None):
    H = lay["SB_HEADS"]
    n_pages = lay["N_PAGES"] if cache is not None else 0
    nl = gain.shape[0]
    width = H * LANES
    cw = math.gcd(lay["SQ_OFF"], width)
    pieces = width // cw

    def ix(f):
        return (lambda b, p, pt: f(b, p, pt)) if n_pages else (lambda b, p: f(b, p, None))

    in_specs = [pl.BlockSpec((nl, LANES), ix(lambda b, p, pt: (0, 0)))]
    in_specs += [pl.BlockSpec((rows, cw), ix((lambda c: lambda b, p, pt: (row_block + b, c))(off // cw + i)))
                 for off in (lay["SQ_OFF"], lay["SK_OFF"], lay["SV_OFF"]) for i in range(pieces)]
    args = [gain] + [z] * (3 * pieces)
    pps = _pages_per_step(n_pages, 2)
    n_steps = n_pages // pps
    if n_pages:
        kc, vc, page_table = cache

        def page(i):
            return lambda b, p, pt: (layer, pt[b, n_pages - jnp.maximum(p, 1) * pps + i], 0, 0)

        in_specs += [pl.BlockSpec((1, 1) + kc.shape[2:], page(i)) for i in range(pps)]
        in_specs += [pl.BlockSpec((1, 1) + vc.shape[2:], page(i)) for i in range(pps)]
        args += [kc] * pps + [vc] * pps
    kernel = functools.partial(_sb_small_kernel, layer=layer, heads=H, pieces=pieces,
                               valid_rows=valid_rows, n_steps=n_steps, pps=pps)
    out_spec = pl.BlockSpec((rows, width), ix(lambda b, p, pt: (b, 0)))
    scratch = [pltpu.VMEM((H * rows, 1), F32), pltpu.VMEM((H * rows, LANES), F32)]
    out_shape = jax.ShapeDtypeStruct((n_seq * rows, width), F32)
    params = _params(("parallel", "arbitrary"), 32 * 1024 * 1024)
    if n_pages:
        return pl.pallas_call(
            kernel,
            grid_spec=pltpu.PrefetchScalarGridSpec(
                num_scalar_prefetch=1, grid=(n_seq, n_steps + 1), in_specs=in_specs,
                out_specs=out_spec, scratch_shapes=scratch),
            out_shape=out_shape, compiler_params=params, name="sb_sample",
        )(page_table, *args)
    return pl.pallas_call(
        kernel, grid=(n_seq, 1), in_specs=in_specs, out_specs=out_spec, scratch_shapes=scratch,
        out_shape=out_shape, compiler_params=params, name="sb_meta",
    )(*args)


def _rope_tables(pos, d_map, rope_dim):
    half = rope_dim // 2
    inv = ROPE_THETA ** (-jnp.arange(half, dtype=F32) / half)
    ang = pos.astype(F32)[:, None] * inv[None, :]
    cos, sin = jnp.cos(ang), jnp.sin(ang)
    rows = pos.shape[0]
    pad = jnp.zeros((rows, d_map - rope_dim), F32)
    zero = jnp.zeros((rows, half), F32)
    c = jnp.concatenate([cos, cos, pad + 1.0], axis=1)
    s1 = jnp.concatenate([-sin, zero, pad], axis=1)
    s2 = jnp.concatenate([zero, sin, pad], axis=1)
    rep = LANES // d_map
    return tuple(jnp.tile(t, (1, rep)) for t in (c, s1, s2))


def kernel(x_prompt, x_sample, cache_diff_k, cache_diff_v, cache_sb_k, cache_sb_v, state_hgrn,
           page_table, meta_tokens, norm_mix, norm_ffn, norm_final, w_in, hg_lower_bounds, hg_norm,
           diff_lambda, diff_subln, sb_norm, w_out, w_gate_up, w_down):
    B, SEQ, D = x_prompt.shape
    DB, DEC_SEQ, _ = x_sample.shape
    DEPTH = w_in.shape[0]
    NM = meta_tokens.shape[0]
    HGH = state_hgrn.shape[2]
    DH = cache_diff_v.shape[3]
    DQK = cache_diff_k.shape[4]
    SH = cache_sb_k.shape[3]
    PAGE = cache_sb_k.shape[2]
    NP = page_table.shape[1]
    assert DEC_SEQ <= SAMPLE_ROWS and NM % 16 == 0 and SEQ % NM == 0 and 2 * DQK == LANES
    assert state_hgrn.shape[3:] == (LANES, LANES) and HGH == DH
    HGW, DW, SW = HGH * LANES, DH * LANES, SH * LANES
    RP, RM, RS = B * SEQ, B * NM, DB * SAMPLE_ROWS
    lay = dict(B=B, SEQ=SEQ, DB=DB, DEC_SEQ=DEC_SEQ, N_META=NM, HG_HEADS=HGH, DIFF_HEADS=DH, SB_HEADS=SH,
               N_PAGES=NP, RP=RP, RM=RM, RS=RS,
               DV_OFF=4 * HGW + 2 * DW, SQ_OFF=4 * HGW + 3 * DW, SK_OFF=4 * HGW + 3 * DW + SW,
               SV_OFF=4 * HGW + 3 * DW + 2 * SW)

    sample = jnp.pad(x_sample, ((0, 0), (0, SAMPLE_ROWS - DEC_SEQ), (0, 0)))
    x = jnp.concatenate([x_prompt.reshape(RP, D),
                         jnp.broadcast_to(meta_tokens[None], (B, NM, D)).reshape(RM, D),
                         sample.reshape(RS, D)], axis=0)
    pos = jnp.concatenate([jnp.tile(NM + jnp.arange(SEQ, dtype=jnp.int32), B),
                           jnp.tile(jnp.arange(NM, dtype=jnp.int32), B),
                           jnp.tile(NP * PAGE + jnp.arange(SAMPLE_ROWS, dtype=jnp.int32), DB)])
    tables = _rope_tables(pos, DQK, DQK // 4)

    ck_d = jnp.transpose(cache_diff_k, (0, 1, 3, 4, 2))
    cv_d = cache_diff_v.reshape(cache_diff_v.shape[:2] + (PAGE * DH, LANES))
    ck_s = cache_sb_k.reshape(cache_sb_k.shape[:2] + (PAGE * SH, LANES))
    cv_s = cache_sb_v.reshape(cache_sb_v.shape[:2] + (PAGE * SH, LANES))

    w_in_b, w_out_b, w_gu_b, w_dn_b = w_in, w_out, w_gate_up, w_down
    g_mix = norm_mix.reshape(DEPTH, 1, D)
    g_ffn = norm_ffn.reshape(DEPTH, 1, D)
    g_fin = norm_final.reshape(1, 1, D)
    meta_blk = RP // NM
    samp_blk = (RP + RM) // SAMPLE_ROWS
    small_meta = dict(row_block=meta_blk, rows=NM, n_seq=B, valid_rows=NM)
    small_samp = dict(row_block=samp_blk, rows=SAMPLE_ROWS, n_seq=DB, valid_rows=DEC_SEQ)

    outs = {k: [] for k in ("pk_d", "pv_d", "pk_s", "pv_s", "ps_h", "sk_d", "sv_d", "sk_s", "sv_s", "ss_h")}

    def cache_rows(a, off, width, heads):
        a = a[:, off:off + width]
        p = jnp.concatenate([a[RP:RP + RM].reshape(B, NM, heads, -1), a[:RP].reshape(B, SEQ, heads, -1)], axis=1)
        s = a[RP + RM:].reshape(DB, SAMPLE_ROWS, heads, -1)[:, :DEC_SEQ]
        return p, s

    for l in range(DEPTH):
        hn = _rmsnorm(x, g_mix, l, BF16)
        z = _matmul(hn, w_in_b, l)
        dqk = _rope(z, 2, 2 * DW, tables)

        hg_m, hg_p, s_p = _hgrn_prompt(z, hg_lower_bounds, hg_norm, l, lay)
        hg_s, s_s = _hgrn_sample(z, state_hgrn, hg_lower_bounds, hg_norm, l, lay)
        df_p = _diff_prompt(dqk, z, diff_lambda, diff_subln, l, lay)
        df_m = _diff_small(dqk, z, diff_lambda, diff_subln, l, lay, **small_meta)
        df_s = _diff_small(dqk, z, diff_lambda, diff_subln, l, lay, **small_samp,
                           cache=(ck_d, cv_d, page_table))
        sb_p = _sb_prompt(z, sb_norm, l, lay)
        sb_m = _sb_small(z, sb_norm, l, lay, **small_meta)
        sb_s = _sb_small(z, sb_norm, l, lay, **small_samp, cache=(ck_s, cv_s, page_table))

        mix = jnp.concatenate([
            jnp.concatenate([hg_p, df_p, sb_p], axis=1),
            jnp.concatenate([hg_m, df_m.astype(BF16), sb_m.astype(BF16)], axis=1),
            jnp.concatenate([hg_s.astype(BF16), df_s.astype(BF16), sb_s.astype(BF16)], axis=1)], axis=0)
        x = _matmul(mix, w_out_b, l, res=x)
        h = _ffn_up(_rmsnorm(x, g_ffn, l, BF16), w_gu_b, l)
        x = _matmul(h, w_dn_b, l, res=x)

        for name_p, name_s, (p, s) in (
                ("pk_d", "sk_d", cache_rows(dqk, DW, DW, 2 * DH)),
                ("pv_d", "sv_d", cache_rows(z, lay["DV_OFF"], DW, DH)),
                ("pk_s", "sk_s", cache_rows(z, lay["SK_OFF"], SW, SH)),
                ("pv_s", "sv_s", cache_rows(z, lay["SV_OFF"], SW, SH))):
            outs[name_p].append(p)
            outs[name_s].append(s)
        outs["ps_h"].append(s_p)
        outs["ss_h"].append(s_s)

    y = _rmsnorm(x, g_fin, 0, F32)
    y_prompt = y[:RP].reshape(B, SEQ, D)
    y_sample = y[RP + RM:].reshape(DB, SAMPLE_ROWS, D)[:, :DEC_SEQ]
    st = lambda k: jnp.stack(outs[k], axis=0)
    return (y_prompt, y_sample, st("pk_d"), st("pv_d"), st("pk_s"), st("pv_s"), st("ps_h"),
            st("sk_d"), st("sv_d"), st("sk_s"), st("sv_s"), st("ss_h"))
```
